```python
import jax, jax.numpy as jnp
from jax import lax
import numpy as np

D_MODEL = 1024
BATCH = 16
SEQ = 2048
DEPTH = 1

CHUNK = 64
MLSTM_HEADS = 4
MLSTM_HEAD_DIM = 256
MLSTM_DIM = MLSTM_HEADS * MLSTM_HEAD_DIM
FOX_HEADS = 16
FOX_HEAD_DIM = 64
FOX_DIM = FOX_HEADS * FOX_HEAD_DIM
Q_BLOCK = 128
FFN_DIM = 2816
CONV_WIDTH = 3
EPS = 1e-6
IN_SPLITS = (MLSTM_DIM, MLSTM_DIM, MLSTM_DIM, MLSTM_DIM, MLSTM_HEADS, MLSTM_HEADS,
             FOX_DIM, FOX_DIM, FOX_DIM, FOX_HEADS, D_MODEL, D_MODEL)
IN_COLS = sum(IN_SPLITS)

kernel_name = "hybrid_mlstm_fox_convffn_block"


def rms_norm(x, w):
    xf = x.astype(jnp.float32)
    y = xf * lax.rsqrt(jnp.mean(xf * xf, axis=-1, keepdims=True) + EPS)
    return (y * w.astype(jnp.float32)).astype(x.dtype)


def modulate(x, w, shift, scale):
    return rms_norm(x, w) * (1 + scale[:, None, :]) + shift[:, None, :]


def mlstm_chunkwise(q, k, v, i_pre, log_f):
    B, S, H, Dh = q.shape
    nc = S // CHUNK

    def heads_chunks(a):
        return a.reshape(B, nc, CHUNK, H, Dh).transpose(1, 0, 3, 2, 4)

    def gates_chunks(a):
        return a.reshape(B, nc, CHUNK, H).transpose(1, 0, 3, 2)

    xs = (heads_chunks(q * Dh ** -0.5), heads_chunks(k), heads_chunks(v),
          gates_chunks(i_pre), gates_chunks(log_f))
    causal = jnp.tril(jnp.ones((CHUNK, CHUNK), dtype=bool))

    def step(carry, chunk):
        C, n, m = carry
        q_, k_, v_, i_, f_ = chunk
        b = jnp.cumsum(f_, axis=-1)
        log_intra = jnp.where(causal, b[..., :, None] - b[..., None, :] + i_[..., None, :], -jnp.inf)
        log_inter = b + m[..., None]
        m_t = jnp.maximum(log_inter, jnp.max(log_intra, axis=-1))
        d_intra = jnp.exp(log_intra - m_t[..., None])
        d_inter = jnp.exp(log_inter - m_t)
        s = jnp.einsum("bhtd,bhsd->bhts", q_, k_) * d_intra
        num = (jnp.einsum("bhts,bhse->bhte", s, v_)
               + d_inter[..., None] * jnp.einsum("bhtd,bhde->bhte", q_, C))
        den = jnp.sum(s, axis=-1) + d_inter * jnp.einsum("bhtd,bhd->bht", q_, n)
        h = num / jnp.maximum(jnp.abs(den), jnp.exp(-m_t))[..., None]
        b_last = b[..., -1]
        log_w = b_last[..., None] - b + i_
        m_new = jnp.maximum(b_last + m, jnp.max(log_w, axis=-1))
        w = jnp.exp(log_w - m_new[..., None])
        decay = jnp.exp(b_last + m - m_new)
        C_new = decay[..., None, None] * C + jnp.einsum("bhs,bhsd,bhse->bhde", w, k_, v_)
        n_new = decay[..., None] * n + jnp.einsum("bhs,bhsd->bhd", w, k_)
        return (C_new, n_new, m_new), h

    init = (jnp.zeros((B, H, Dh, Dh), jnp.float32),
            jnp.zeros((B, H, Dh), jnp.float32),
            jnp.zeros((B, H), jnp.float32))
    _, h = lax.scan(step, init, xs)
    return h.transpose(1, 0, 3, 2, 4).reshape(B, S, H, Dh)


def forgetting_attention(q, k, v, log_f):
    S = q.shape[2]
    Fcum = jnp.cumsum(log_f, axis=-1)
    scale = q.shape[-1] ** -0.5
    outs = []
    for blk in range(S // Q_BLOCK):
        q0, q1 = blk * Q_BLOCK, (blk + 1) * Q_BLOCK
        logits = jnp.einsum("bhqd,bhkd->bhqk", q[:, :, q0:q1], k[:, :, :q1]).astype(jnp.float32) * scale
        logits = logits + Fcum[:, :, q0:q1, None] - Fcum[:, :, None, :q1]
        mask = jnp.arange(q0, q1)[:, None] >= jnp.arange(q1)[None, :]
        p = jax.nn.softmax(jnp.where(mask, logits, -jnp.inf), axis=-1)
        outs.append(jnp.einsum("bhqk,bhkd->bhqd", p.astype(v.dtype), v[:, :, :q1]))
    return jnp.concatenate(outs, axis=2)


def setup_inputs(seed: int = 0) -> dict:
    key = jax.random.key(seed)
    ks = jax.random.split(key, 24)

    def nrm(k, shape, scale):
        return jax.random.normal(k, shape, jnp.float32) * scale

    L = DEPTH
    return {
        "x": nrm(ks[0], (BATCH, SEQ, D_MODEL), 1.0),
        "c": nrm(ks[1], (BATCH, D_MODEL), 1.0),
        "w_ada": nrm(ks[2], (L, D_MODEL, 6 * D_MODEL), 0.02),
        "b_ada": nrm(ks[3], (L, 6 * D_MODEL), 0.01),
        "norm1_w": 1.0 + nrm(ks[4], (L, D_MODEL), 0.02),
        "w_in": nrm(ks[5], (L, D_MODEL, IN_COLS), D_MODEL ** -0.5),
        "b_mlstm_i": nrm(ks[6], (L, MLSTM_HEADS), 0.1),
        "b_mlstm_f": jnp.linspace(3.0, 6.0, MLSTM_HEADS, dtype=jnp.float32) + nrm(ks[7], (L, MLSTM_HEADS), 0.1),
        "mlstm_norm_w": 1.0 + nrm(ks[8], (L, MLSTM_DIM), 0.02),
        "b_fox_f": jnp.linspace(1.0, 5.0, FOX_HEADS, dtype=jnp.float32) + nrm(ks[9], (L, FOX_HEADS), 0.1),
        "fox_q_norm_w": 1.0 + nrm(ks[10], (L, FOX_HEAD_DIM), 0.02),
        "fox_k_norm_w": 1.0 + nrm(ks[11], (L, FOX_HEAD_DIM), 0.02),
        "w_branch_mlstm": nrm(ks[12], (L, MLSTM_DIM, D_MODEL), MLSTM_DIM ** -0.5),
        "w_branch_fox": nrm(ks[13], (L, FOX_DIM, D_MODEL), FOX_DIM ** -0.5),
        "w_out": nrm(ks[14], (L, D_MODEL, D_MODEL), D_MODEL ** -0.5),
        "norm2_w": 1.0 + nrm(ks[15], (L, D_MODEL), 0.02),
        "w_up": nrm(ks[16], (L, D_MODEL, 2 * FFN_DIM), D_MODEL ** -0.5),
        "conv_w": nrm(ks[17], (L, CONV_WIDTH, 2 * FFN_DIM), CONV_WIDTH ** -0.5),
        "conv_b": nrm(ks[18], (L, 2 * FFN_DIM), 0.01),
        "w_down": nrm(ks[19], (L, FFN_DIM, D_MODEL), FFN_DIM ** -0.5),
    }


def reference(x, c, w_ada, b_ada, norm1_w, w_in, b_mlstm_i, b_mlstm_f, mlstm_norm_w,
              b_fox_f, fox_q_norm_w, fox_k_norm_w, w_branch_mlstm, w_branch_fox, w_out,
              norm2_w, w_up, conv_w, conv_b, w_down):
    B, S, _ = x.shape
    f32 = jnp.float32
    split_points = np.cumsum(IN_SPLITS)[:-1].tolist()
    for l in range(DEPTH):
        ada = jnp.dot(jax.nn.silu(c), w_ada[l]) + b_ada[l]
        shift1, scale1, gate1, shift2, scale2, gate2 = jnp.split(ada, 6, axis=-1)

        h = modulate(x, norm1_w[l], shift1, scale1)
        proj = jnp.einsum("bsd,de->bse", h, w_in[l])
        mq, mk, mv, mo, mi, mf, fq, fk, fv, ff, ga, gb = jnp.split(proj, split_points, axis=-1)

        i_pre = (mi + b_mlstm_i[l]).astype(f32)
        m_log_f = jax.nn.log_sigmoid((mf + b_mlstm_f[l]).astype(f32))
        hm = mlstm_chunkwise(mq.reshape(B, S, MLSTM_HEADS, MLSTM_HEAD_DIM).astype(f32),
                             mk.reshape(B, S, MLSTM_HEADS, MLSTM_HEAD_DIM).astype(f32),
                             mv.reshape(B, S, MLSTM_HEADS, MLSTM_HEAD_DIM).astype(f32),
                             i_pre, m_log_f)
        hm = rms_norm(hm, mlstm_norm_w[l].reshape(MLSTM_HEADS, MLSTM_HEAD_DIM))
        hm = (hm.reshape(B, S, MLSTM_DIM) * jax.nn.sigmoid(mo.astype(f32))).astype(x.dtype)

        fq = rms_norm(fq.reshape(B, S, FOX_HEADS, FOX_HEAD_DIM), fox_q_norm_w[l]).transpose(0, 2, 1, 3)
        fk = rms_norm(fk.reshape(B, S, FOX_HEADS, FOX_HEAD_DIM), fox_k_norm_w[l]).transpose(0, 2, 1, 3)
        fv = fv.reshape(B, S, FOX_HEADS, FOX_HEAD_DIM).transpose(0, 2, 1, 3)
        fox_log_f = jax.nn.log_sigmoid((ff + b_fox_f[l]).astype(f32)).transpose(0, 2, 1)
        hf = forgetting_attention(fq, fk, fv, fox_log_f).transpose(0, 2, 1, 3).reshape(B, S, FOX_DIM)

        ya = jnp.einsum("bse,ed->bsd", hm, w_branch_mlstm[l])
        yb = jnp.einsum("bse,ed->bsd", hf, w_branch_fox[l])
        merged = jax.nn.sigmoid(ga) * ya + jax.nn.sigmoid(gb) * yb
        x = x + gate1[:, None, :] * jnp.einsum("bsd,de->bse", merged, w_out[l])

        h2 = modulate(x, norm2_w[l], shift2, scale2)
        u = jnp.einsum("bsd,df->bsf", h2, w_up[l])
        u = lax.conv_general_dilated(u, conv_w[l][:, None, :], window_strides=(1,),
                                     padding=[(CONV_WIDTH - 1, 0)],
                                     dimension_numbers=("NWC", "WIO", "NWC"),
                                     feature_group_count=2 * FFN_DIM) + conv_b[l]
        u_gate, u_val = jnp.split(u, 2, axis=-1)
        y2 = jnp.einsum("bsf,fd->bsd", jax.nn.silu(u_gate) * u_val, w_down[l])
        x = x + gate2[:, None, :] * y2
    return x
```

```python
import functools

import jax
import jax.numpy as jnp
from jax import lax
from jax.experimental import pallas as pl
from jax.experimental.pallas import tpu as pltpu

D_MODEL = 1024
MLSTM_HEADS = 4
MLSTM_HEAD_DIM = 256
FOX_HEADS = 16
FOX_HEAD_DIM = 64
FFN_DIM = 2816
CONV_WIDTH = 3
EPS = 1e-6

LANES = 128
SUBLANES = 8
N_BIG = 9
GATE_COLS = LANES
MLSTM_CHUNK = 256
FOX_TILE = 256
FOX_PAIR = LANES // FOX_HEAD_DIM
VMEM_LIMIT = 56 * 1024 * 1024

f32 = jnp.float32
bf16 = jnp.bfloat16


def _dot(a, b):
    return jnp.dot(a, b, preferred_element_type=f32)


def _dot_nt(a, b):
    return lax.dot_general(a, b, (((1,), (1,)), ((), ())), preferred_element_type=f32)


def _dot_tn(a, b):
    return lax.dot_general(a, b, (((0,), (0,)), ((), ())), preferred_element_type=f32)


def _split3(a):
    a1 = a.astype(bf16)
    r1 = a - a1.astype(f32)
    a2 = r1.astype(bf16)
    a3 = (r1 - a2.astype(f32)).astype(bf16)
    return a1, a2, a3


def _log_sigmoid(x):
    return jnp.minimum(x, 0.0) - jnp.log1p(jnp.exp(-jnp.abs(x)))


def _rms_modulate(x, norm_w, shift, scale):
    y = x * lax.rsqrt(jnp.mean(x * x, axis=-1, keepdims=True) + EPS)
    return (y * norm_w) * (1.0 + scale) + shift


def _ada_kernel(c_ref, w_ref, b_ref, out_ref):
    c = c_ref[...]
    a = (c * jax.nn.sigmoid(c)).astype(bf16)
    out_ref[...] = _dot(a, w_ref[...].astype(bf16)) + b_ref[...]


def _ada(c, w_ada, b_ada):
    B, D = c.shape
    n = w_ada.shape[1] // D
    return pl.pallas_call(
        _ada_kernel,
        grid=(n,),
        in_specs=[pl.BlockSpec((B, D), lambda j: (0, 0)),
                  pl.BlockSpec((D, D), lambda j: (0, j)),
                  pl.BlockSpec((1, D), lambda j: (0, j))],
        out_specs=pl.BlockSpec((B, D), lambda j: (0, j)),
        out_shape=jax.ShapeDtypeStruct((B, n * D), f32),
        name="ada",
    )(c, w_ada, b_ada)


def _in_proj_kernel(x_ref, ada_ref, nw_ref, w_ref, wg_ref, bg_ref, p_ref, g_ref, h_ref):
    j = pl.program_id(1)

    @pl.when(j == 0)
    def _():
        h = _rms_modulate(x_ref[...], nw_ref[...], ada_ref[0:1, :], ada_ref[1:2, :]).astype(bf16)
        h_ref[...] = h
        g = _dot(h, wg_ref[...]) + bg_ref[...]
        col = lax.broadcasted_iota(jnp.int32, g.shape, 1)
        g_ref[...] = jnp.where(col < MLSTM_HEADS, g, _log_sigmoid(g))

    p_ref[...] = _dot(h_ref[...], w_ref[...]).astype(bf16)


def _in_proj(x2, ada3, norm_w, w_big, w_gate, b_gate, seq, tm):
    M, D = x2.shape
    tiles_per_seq = seq // tm
    return pl.pallas_call(
        _in_proj_kernel,
        grid=(M // tm, N_BIG),
        in_specs=[pl.BlockSpec((tm, D), lambda m, j: (m, 0)),
                  pl.BlockSpec((None, 6, D), lambda m, j: (m // tiles_per_seq, 0, 0)),
                  pl.BlockSpec((1, D), lambda m, j: (0, 0)),
                  pl.BlockSpec((D, D), lambda m, j: (0, j)),
                  pl.BlockSpec((D, GATE_COLS), lambda m, j: (0, 0)),
                  pl.BlockSpec((1, GATE_COLS), lambda m, j: (0, 0))],
        out_specs=[pl.BlockSpec((None, tm, D), lambda m, j: (j, m, 0)),
                   pl.BlockSpec((tm, GATE_COLS), lambda m, j: (m, 0))],
        out_shape=[jax.ShapeDtypeStruct((N_BIG, M, D), bf16),
                   jax.ShapeDtypeStruct((M, GATE_COLS), f32)],
        scratch_shapes=[pltpu.VMEM((tm, D), bf16)],
        compiler_params=pltpu.CompilerParams(
            dimension_semantics=("arbitrary", "arbitrary"), vmem_limit_bytes=VMEM_LIMIT),
        name="in_proj",
    )(x2, ada3, norm_w, w_big, w_gate, b_gate)


def _fox_cum_kernel(g_ref, f_ref):
    S = g_ref.shape[0]
    T = FOX_TILE
    gt = g_ref[...].T
    lf = gt[2 * MLSTM_HEADS:2 * MLSTM_HEADS + FOX_HEADS, :]
    row = lax.broadcasted_iota(jnp.int32, (T, T), 0)
    col = lax.broadcasted_iota(jnp.int32, (T, T), 1)
    upper = (row <= col).astype(bf16)
    carry = jnp.zeros((FOX_HEADS, 1), f32)
    for i in range(S // T):
        a1, a2, a3 = _split3(lf[:, i * T:(i + 1) * T])
        cs = (_dot(a1, upper) + _dot(a2, upper)) + _dot(a3, upper) + carry
        f_ref[:, i * T:(i + 1) * T] = cs
        carry = cs[:, T - 1:T]


def _fox_cum(g, batch, seq):
    return pl.pallas_call(
        _fox_cum_kernel,
        grid=(batch,),
        in_specs=[pl.BlockSpec((seq, GATE_COLS), lambda b: (b, 0))],
        out_specs=pl.BlockSpec((None, FOX_HEADS, seq), lambda b: (b, 0, 0)),
        out_shape=jax.ShapeDtypeStruct((batch, FOX_HEADS, seq), f32),
        name="fox_cum",
    )(g)


def _mlstm_kernel(q_ref, k_ref, v_ref, o_ref, g_ref, nw_ref, out_ref, c_ref, n_ref, m_ref):
    L = MLSTM_CHUNK
    Dh = MLSTM_HEAD_DIM

    @pl.when(pl.program_id(1) == 0)
    def _():
        c_ref[...] = jnp.zeros_like(c_ref)
        n_ref[...] = jnp.zeros_like(n_ref)
        m_ref[...] = jnp.zeros_like(m_ref)

    row = lax.broadcasted_iota(jnp.int32, (L, L), 0)
    col = lax.broadcasted_iota(jnp.int32, (L, L), 1)
    causal = col <= row
    lower = causal.astype(bf16)
    upper = (row <= col).astype(bf16)

    g = g_ref[...]
    g1, g2, g3 = _split3(g)
    b_cols = (_dot(lower, g1) + _dot(lower, g2)) + _dot(lower, g3)
    gt = g.T[0:SUBLANES, :]
    t1, t2, t3 = _split3(gt)
    b_rows = (_dot(t1, upper) + _dot(t2, upper)) + _dot(t3, upper)

    for h in range(MLSTM_HEADS):
        hs = slice(h * Dh, (h + 1) * Dh)
        i_col = g[:, h:h + 1]
        b_col = b_cols[:, MLSTM_HEADS + h:MLSTM_HEADS + h + 1]
        a_row = gt[h:h + 1, :] - b_rows[MLSTM_HEADS + h:MLSTM_HEADS + h + 1, :]
        m_prev = m_ref[h][0:1, 0:1]
        c_prev = c_ref[h]
        n_prev = n_ref[h]

        log_intra = jnp.where(causal, b_col + a_row, -jnp.inf)
        log_inter = b_col + m_prev
        m_t = jnp.maximum(log_inter, jnp.max(log_intra, axis=1, keepdims=True))
        d_intra = jnp.exp(log_intra - m_t)
        d_inter = jnp.exp(log_inter - m_t)

        qh = q_ref[:, hs] * jnp.asarray(Dh ** -0.5, bf16)
        kh = k_ref[:, hs]
        vh = v_ref[:, hs]
        s = _dot_nt(qh, kh) * d_intra
        num = _dot(s.astype(bf16), vh) + d_inter * _dot(qh, c_prev.astype(bf16))
        qn = jnp.sum(qh.astype(f32) * n_prev, axis=1, keepdims=True)
        den = jnp.sum(s, axis=1, keepdims=True) + d_inter * qn
        hh = num / jnp.maximum(jnp.abs(den), jnp.exp(-m_t))

        y = hh * lax.rsqrt(jnp.mean(hh * hh, axis=1, keepdims=True) + EPS) * nw_ref[:, hs]
        y = y * jax.nn.sigmoid(o_ref[:, hs].astype(f32))
        out_ref[:, hs] = y.astype(bf16)

        b_last = b_col[L - 1:L, :]
        log_w = b_last - b_col + i_col
        m_new = jnp.maximum(b_last + m_prev, jnp.max(log_w, axis=0, keepdims=True))
        w = jnp.exp(log_w - m_new)
        decay = jnp.exp(b_last + m_prev - m_new)
        kw = kh.astype(f32) * w
        c_ref[h] = decay * c_prev + _dot_tn(kw.astype(bf16), vh)
        n_ref[h] = decay * n_prev + jnp.sum(kw, axis=0, keepdims=True)
        m_ref[h] = jnp.broadcast_to(m_new, m_ref.shape[1:])


def _mlstm(p, g, norm_w, batch, seq):
    M, D = p.shape[1], p.shape[2]
    L = MLSTM_CHUNK
    nc = seq // L

    def pspec(idx):
        return pl.BlockSpec((None, L, D), lambda b, c: (idx, b * nc + c, 0))

    return pl.pallas_call(
        _mlstm_kernel,
        grid=(batch, nc),
        in_specs=[pspec(0), pspec(1), pspec(2), pspec(3),
                  pl.BlockSpec((L, GATE_COLS), lambda b, c: (b * nc + c, 0)),
                  pl.BlockSpec((1, D), lambda b, c: (0, 0))],
        out_specs=pl.BlockSpec((L, D), lambda b, c: (b * nc + c, 0)),
        out_shape=jax.ShapeDtypeStruct((M, D), bf16),
        scratch_shapes=[pltpu.VMEM((MLSTM_HEADS, MLSTM_HEAD_DIM, MLSTM_HEAD_DIM), f32),
                        pltpu.VMEM((MLSTM_HEADS, 1, MLSTM_HEAD_DIM), f32),
                        pltpu.VMEM((MLSTM_HEADS, SUBLANES, LANES), f32)],
        compiler_params=pltpu.CompilerParams(
            dimension_semantics=("arbitrary", "arbitrary"), vmem_limit_bytes=VMEM_LIMIT),
        name="mlstm",
    )(p, p, p, p, g, norm_w)


def _fox_kernel(q_ref, k_ref, v_ref, f_ref, qw_ref, kw_ref, out_ref, qn_ref, kn_ref):
    T = FOX_TILE
    Dh = FOX_HEAD_DIM
    pair = pl.program_id(1)
    i = pl.program_id(2)

    @pl.when(i == 0)
    def _():
        r = lax.broadcasted_iota(jnp.int32, (LANES, LANES), 0) // Dh
        c = lax.broadcasted_iota(jnp.int32, (LANES, LANES), 1) // Dh
        same_head = (r == c).astype(bf16)

        def head_rms(a, w):
            sq = a * a
            s1 = sq.astype(bf16)
            s2 = (sq - s1.astype(f32)).astype(bf16)
            ms = (_dot(s1, same_head) + _dot(s2, same_head)) * (1.0 / Dh)
            return a * lax.rsqrt(ms + EPS) * w

        qn_ref[...] = (head_rms(q_ref[...].astype(f32), qw_ref[...]) * (Dh ** -0.5)).astype(bf16)
        kn_ref[...] = head_rms(k_ref[...].astype(f32), kw_ref[...]).astype(bf16)

    q0 = pl.multiple_of(i * T, T)
    q = qn_ref[pl.ds(q0, T), :]
    lane = lax.broadcasted_iota(jnp.int32, (T, LANES), 1)
    head_of_lane = lane // Dh
    qs = [jnp.where(head_of_lane == hh, q, jnp.zeros_like(q)) for hh in range(FOX_PAIR)]

    def tile(j, carry, masked):
        k0 = pl.multiple_of(j * T, T)
        kt = kn_ref[pl.ds(k0, T), :]
        vt = v_ref[pl.ds(k0, T), :]
        out = []
        for hh in range(FOX_PAIR):
            m, l, acc = carry[hh]
            neg_f = -f_ref[pl.ds(pair * FOX_PAIR + hh, 1), pl.ds(k0, T)]
            s = _dot_nt(qs[hh], kt) + neg_f
            if masked:
                r = lax.broadcasted_iota(jnp.int32, (T, T), 0)
                c = lax.broadcasted_iota(jnp.int32, (T, T), 1)
                s = jnp.where(c <= r, s, -jnp.inf)
            m_new = jnp.maximum(m, jnp.max(s, axis=1, keepdims=True))
            alpha = jnp.exp(m - m_new)
            p = jnp.exp(s - m_new)
            l = alpha * l + jnp.sum(p, axis=1, keepdims=True)
            acc = alpha * acc + _dot(p.astype(bf16), vt)
            out.append((m_new, l, acc))
        return tuple(out)

    init = tuple((jnp.full((T, 1), -jnp.inf, f32), jnp.zeros((T, 1), f32), jnp.zeros((T, LANES), f32))
                 for _ in range(FOX_PAIR))
    carry = lax.fori_loop(0, i, lambda j, cr: tile(j, cr, False), init)
    carry = tile(i, carry, True)
    res = [acc / l for (_, l, acc) in carry]
    o = res[0]
    for hh in range(1, FOX_PAIR):
        o = jnp.where(head_of_lane == hh, res[hh], o)
    out_ref[...] = o.astype(bf16)


def _fox(p, fcum, q_norm_w, k_norm_w, batch, seq):
    M, D = p.shape[1], p.shape[2]
    T = FOX_TILE
    nq = seq // T
    npairs = FOX_HEADS // FOX_PAIR

    def pspec(idx):
        return pl.BlockSpec((None, seq, LANES), lambda b, pr, i: (idx, b, pr))

    return pl.pallas_call(
        _fox_kernel,
        grid=(batch, npairs, nq),
        in_specs=[pspec(4), pspec(5), pspec(6),
                  pl.BlockSpec((None, FOX_HEADS, seq), lambda b, pr, i: (b, 0, 0)),
                  pl.BlockSpec((1, LANES), lambda b, pr, i: (0, 0)),
                  pl.BlockSpec((1, LANES), lambda b, pr, i: (0, 0))],
        out_specs=pl.BlockSpec((T, LANES), lambda b, pr, i: (b * nq + i, pr)),
        out_shape=jax.ShapeDtypeStruct((M, D), bf16),
        scratch_shapes=[pltpu.VMEM((seq, LANES), bf16), pltpu.VMEM((seq, LANES), bf16)],
        compiler_params=pltpu.CompilerParams(
            dimension_semantics=("arbitrary", "arbitrary", "arbitrary"), vmem_limit_bytes=VMEM_LIMIT),
        name="fox",
    )(p, p, p, fcum, q_norm_w, k_norm_w)


def _merge_kernel(x_ref, ada_ref, hm_ref, hf_ref, ga_ref, gb_ref, wa_ref, wb_ref, wo_ref, out_ref):
    ya = _dot(hm_ref[...], wa_ref[...])
    yb = _dot(hf_ref[...], wb_ref[...])
    merged = (jax.nn.sigmoid(ga_ref[...].astype(f32)) * ya
              + jax.nn.sigmoid(gb_ref[...].astype(f32)) * yb)
    y = _dot(merged.astype(bf16), wo_ref[...])
    out_ref[...] = x_ref[...] + ada_ref[2:3, :] * y


def _merge(x2, ada3, hm, hf, p, w_a, w_b, w_o, seq, tm):
    M, D = x2.shape
    tiles_per_seq = seq // tm
    row = pl.BlockSpec((tm, D), lambda m: (m, 0))
    wspec = pl.BlockSpec((D, D), lambda m: (0, 0), pipeline_mode=pl.Buffered(1))
    return pl.pallas_call(
        _merge_kernel,
        grid=(M // tm,),
        in_specs=[row,
                  pl.BlockSpec((None, 6, D), lambda m: (m // tiles_per_seq, 0, 0)),
                  row, row,
                  pl.BlockSpec((None, tm, D), lambda m: (7, m, 0)),
                  pl.BlockSpec((None, tm, D), lambda m: (8, m, 0)),
                  wspec, wspec, wspec],
        out_specs=row,
        out_shape=jax.ShapeDtypeStruct((M, D), f32),
        compiler_params=pltpu.CompilerParams(
            dimension_semantics=("arbitrary",), vmem_limit_bytes=VMEM_LIMIT),
        name="merge",
    )(x2, ada3, hm, hf, p, p, w_a, w_b, w_o)


def _ffn_kernel(x_ref, ada_ref, nw_ref, wup_ref, cw_ref, cb_ref, wdown_ref, out_ref,
                h_ref, ubuf_ref, tail_ref, acc_ref, *, tiles_per_seq, tf):
    tm = x_ref.shape[0]
    H = SUBLANES
    first = (pl.program_id(0) % tiles_per_seq) == 0
    x = x_ref[...]
    h_ref[...] = _rms_modulate(x, nw_ref[...], ada_ref[3:4, :], ada_ref[4:5, :]).astype(bf16)

    def conv_cols(c0):
        cols = slice(c0, c0 + tf)
        u = _dot(h_ref[...], wup_ref[:, cols])
        ubuf_ref[H:H + tm, :] = u

        @pl.when(first)
        def _():
            ubuf_ref[0:H, :] = jnp.zeros((H, tf), f32)

        @pl.when(jnp.logical_not(first))
        def _():
            ubuf_ref[0:H, :] = tail_ref[:, cols]

        tail_ref[:, cols] = u[tm - H:tm, :]
        return (u * cw_ref[2:3, cols] + ubuf_ref[H - 1:H - 1 + tm, :] * cw_ref[1:2, cols]
                + ubuf_ref[H - 2:H - 2 + tm, :] * cw_ref[0:1, cols] + cb_ref[:, cols])

    for f in range(FFN_DIM // tf):
        ug = conv_cols(f * tf)
        uv = conv_cols(FFN_DIM + f * tf)
        a = ((ug * jax.nn.sigmoid(ug)) * uv).astype(bf16)
        y = _dot(a, wdown_ref[f * tf:(f + 1) * tf, :])
        if f == 0:
            acc_ref[...] = y
        else:
            acc_ref[...] += y
    out_ref[...] = x + ada_ref[5:6, :] * acc_ref[...]


def _ffn(x2, ada3, norm_w, w_up, conv_w, conv_b, w_down, seq, tm, tf):
    M, D = x2.shape
    F2 = w_up.shape[1]
    tiles_per_seq = seq // tm
    row = pl.BlockSpec((tm, D), lambda m: (m, 0))

    def const(shape):
        return pl.BlockSpec(shape, lambda m: (0, 0), pipeline_mode=pl.Buffered(1))

    return pl.pallas_call(
        functools.partial(_ffn_kernel, tiles_per_seq=tiles_per_seq, tf=tf),
        grid=(M // tm,),
        in_specs=[row,
                  pl.BlockSpec((None, 6, D), lambda m: (m // tiles_per_seq, 0, 0)),
                  const((1, D)), const((D, F2)), const((CONV_WIDTH, F2)), const((1, F2)),
                  const((F2 // 2, D))],
        out_specs=row,
        out_shape=jax.ShapeDtypeStruct((M, D), f32),
        scratch_shapes=[pltpu.VMEM((tm, D), bf16),
                        pltpu.VMEM((tm + SUBLANES, tf), f32),
                        pltpu.VMEM((SUBLANES, F2), f32),
                        pltpu.VMEM((tm, D), f32)],
        compiler_params=pltpu.CompilerParams(
            dimension_semantics=("arbitrary",), vmem_limit_bytes=VMEM_LIMIT),
        name="ffn",
    )(x2, ada3, norm_w, w_up, conv_w, conv_b, w_down)


def _layer(x2, c, batch, seq, w_ada, b_ada, norm1_w, w_in, b_mlstm_i, b_mlstm_f, mlstm_norm_w,
           b_fox_f, fox_q_norm_w, fox_k_norm_w, w_branch_mlstm, w_branch_fox, w_out,
           norm2_w, w_up, conv_w, conv_b, w_down):
    D = D_MODEL
    ada3 = _ada(c, w_ada, b_ada.reshape(1, -1)).reshape(batch, 6, D)

    o_mi = 4 * D
    o_mf = o_mi + MLSTM_HEADS
    o_fq = o_mf + MLSTM_HEADS
    o_ff = o_fq + 3 * D
    o_ga = o_ff + FOX_HEADS
    w_big = jnp.concatenate([w_in[:, :o_mi], w_in[:, o_fq:o_ff], w_in[:, o_ga:]], axis=1).astype(bf16)
    n_gate = 2 * MLSTM_HEADS + FOX_HEADS
    w_gate = jnp.concatenate([w_in[:, o_mi:o_fq], w_in[:, o_ff:o_ga],
                              jnp.zeros((D, GATE_COLS - n_gate), f32)], axis=1).astype(bf16)
    b_gate = jnp.concatenate([b_mlstm_i, b_mlstm_f, b_fox_f,
                              jnp.zeros((GATE_COLS - n_gate,), f32)]).reshape(1, GATE_COLS)

    p, g = _in_proj(x2, ada3, norm1_w.reshape(1, D), w_big, w_gate, b_gate, seq, tm=1024)
    fcum = _fox_cum(g, batch, seq)
    hm = _mlstm(p, g, mlstm_norm_w.reshape(1, D), batch, seq)
    hf = _fox(p, fcum, jnp.tile(fox_q_norm_w, FOX_PAIR).reshape(1, LANES),
              jnp.tile(fox_k_norm_w, FOX_PAIR).reshape(1, LANES), batch, seq)
    x2 = _merge(x2, ada3, hm, hf, p, w_branch_mlstm.astype(bf16), w_branch_fox.astype(bf16),
                w_out.astype(bf16), seq, tm=512)
    return _ffn(x2, ada3, norm2_w.reshape(1, D), w_up.astype(bf16), conv_w, conv_b.reshape(1, -1),
                w_down.astype(bf16), seq, tm=512, tf=256)


def kernel(x, c, w_ada, b_ada, norm1_w, w_in, b_mlstm_i, b_mlstm_f, mlstm_norm_w, b_fox_f,
           fox_q_norm_w, fox_k_norm_w, w_branch_mlstm, w_branch_fox, w_out, norm2_w, w_up,
           conv_w, conv_b, w_down):
    batch, seq, D = x.shape
    x2 = x.reshape(batch * seq, D)
    for l in range(w_ada.shape[0]):
        x2 = _layer(x2, c, batch, seq, w_ada[l], b_ada[l], norm1_w[l], w_in[l], b_mlstm_i[l],
                    b_mlstm_f[l], mlstm_norm_w[l], b_fox_f[l], fox_q_norm_w[l], fox_k_norm_w[l],
                    w_branch_mlstm[l], w_branch_fox[l], w_out[l], norm2_w[l], w_up[l],
                    conv_w[l], conv_b[l], w_down[l])
    return x2.reshape(batch, seq, D)
```

```python
import functools

import jax
import jax.numpy as jnp
from jax import lax
from jax.experimental import pallas as pl
from jax.experimental.pallas import tpu as pltpu

D_MODEL = 1024
MLSTM_HEADS = 4
MLSTM_HEAD_DIM = 256
FOX_HEADS = 16
FOX_HEAD_DIM = 64
FFN_DIM = 2816
CONV_WIDTH = 3
EPS = 1e-6

LANES = 128
SUBLANES = 8
N_BIG = 9
GATE_COLS = LANES
MLSTM_CHUNK = 256
FOX_TILE = 256
FOX_PAIR = LANES // FOX_HEAD_DIM
FOX_VROWS = FOX_HEAD_DIM + 16
FOX_AROWS = FOX_HEAD_DIM + SUBLANES
FOX_LOOKAHEAD = 4
VMEM_LIMIT = 56 * 1024 * 1024
LOG2E = 1.4426950408889634

f32 = jnp.float32
bf16 = jnp.bfloat16


def _dot(a, b):
    return jnp.dot(a, b, preferred_element_type=f32)


def _dot_nt(a, b):
    return lax.dot_general(a, b, (((1,), (1,)), ((), ())), preferred_element_type=f32)


def _dot_tn(a, b):
    return lax.dot_general(a, b, (((0,), (0,)), ((), ())), preferred_element_type=f32)


def _split3(a):
    a1 = a.astype(bf16)
    r1 = a - a1.astype(f32)
    a2 = r1.astype(bf16)
    a3 = (r1 - a2.astype(f32)).astype(bf16)
    return a1, a2, a3


def _log_sigmoid(x):
    return jnp.minimum(x, 0.0) - jnp.log1p(jnp.exp(-jnp.abs(x)))


def _rms_modulate(x, norm_w, shift, scale):
    y = x * lax.rsqrt(jnp.mean(x * x, axis=-1, keepdims=True) + EPS)
    return (y * norm_w) * (1.0 + scale) + shift


def _ada_kernel(c_ref, w_ref, b_ref, out_ref):
    c = c_ref[...]
    a = (c * jax.nn.sigmoid(c)).astype(bf16)
    out_ref[...] = _dot(a, w_ref[...].astype(bf16)) + b_ref[...]


def _ada(c, w_ada, b_ada):
    B, D = c.shape
    n = w_ada.shape[1] // D
    return pl.pallas_call(
        _ada_kernel,
        grid=(n,),
        in_specs=[pl.BlockSpec((B, D), lambda j: (0, 0)),
                  pl.BlockSpec((D, D), lambda j: (0, j)),
                  pl.BlockSpec((1, D), lambda j: (0, j))],
        out_specs=pl.BlockSpec((B, D), lambda j: (0, j)),
        out_shape=jax.ShapeDtypeStruct((B, n * D), f32),
        name="ada",
    )(c, w_ada, b_ada)


def _in_proj_kernel(x_ref, ada_ref, nw_ref, w_ref, wg_ref, bg_ref, p_ref, g_ref, h_ref):
    j = pl.program_id(1)

    @pl.when(j == 0)
    def _():
        h = _rms_modulate(x_ref[...], nw_ref[...], ada_ref[0:1, :], ada_ref[1:2, :]).astype(bf16)
        h_ref[...] = h
        g = _dot(h, wg_ref[...]) + bg_ref[...]
        col = lax.broadcasted_iota(jnp.int32, g.shape, 1)
        g_ref[...] = jnp.where(col < MLSTM_HEADS, g, _log_sigmoid(g))

    p_ref[...] = _dot(h_ref[...], w_ref[...]).astype(bf16)


def _in_proj(x2, ada3, norm_w, w_big, w_gate, b_gate, seq, tm):
    M, D = x2.shape
    tiles_per_seq = seq // tm
    return pl.pallas_call(
        _in_proj_kernel,
        grid=(M // tm, N_BIG),
        in_specs=[pl.BlockSpec((tm, D), lambda m, j: (m, 0)),
                  pl.BlockSpec((None, 6, D), lambda m, j: (m // tiles_per_seq, 0, 0)),
                  pl.BlockSpec((1, D), lambda m, j: (0, 0)),
                  pl.BlockSpec((D, D), lambda m, j: (0, j)),
                  pl.BlockSpec((D, GATE_COLS), lambda m, j: (0, 0)),
                  pl.BlockSpec((1, GATE_COLS), lambda m, j: (0, 0))],
        out_specs=[pl.BlockSpec((None, tm, D), lambda m, j: (j, m, 0)),
                   pl.BlockSpec((tm, GATE_COLS), lambda m, j: (m, 0))],
        out_shape=[jax.ShapeDtypeStruct((N_BIG, M, D), bf16),
                   jax.ShapeDtypeStruct((M, GATE_COLS), f32)],
        scratch_shapes=[pltpu.VMEM((tm, D), bf16)],
        compiler_params=pltpu.CompilerParams(
            dimension_semantics=("arbitrary", "arbitrary"), vmem_limit_bytes=VMEM_LIMIT),
        name="in_proj",
    )(x2, ada3, norm_w, w_big, w_gate, b_gate)


def _fox_cum_kernel(g_ref, f_ref):
    S = g_ref.shape[0]
    T = FOX_TILE
    row = lax.broadcasted_iota(jnp.int32, (T, T), 0)
    col = lax.broadcasted_iota(jnp.int32, (T, T), 1)
    lower = (col <= row).astype(bf16)
    carry = jnp.zeros((1, GATE_COLS), f32)
    for i in range(S // T):
        a1, a2, a3 = _split3(g_ref[i * T:(i + 1) * T, :])
        cs = (_dot(lower, a1) + _dot(lower, a2)) + _dot(lower, a3) + carry
        f_ref[i * T:(i + 1) * T, :] = cs
        carry = cs[T - 1:T, :]


def _fox_cum(g, batch, seq):
    return pl.pallas_call(
        _fox_cum_kernel,
        grid=(batch,),
        in_specs=[pl.BlockSpec((seq, GATE_COLS), lambda b: (b, 0))],
        out_specs=pl.BlockSpec((seq, GATE_COLS), lambda b: (b, 0)),
        out_shape=jax.ShapeDtypeStruct((batch * seq, GATE_COLS), f32),
        name="fox_cum",
    )(g)


def _mlstm_kernel(q_ref, k_ref, v_ref, o_ref, g_ref, nw_ref, out_ref, c_ref, n_ref, m_ref):
    L = MLSTM_CHUNK
    Dh = MLSTM_HEAD_DIM

    @pl.when(pl.program_id(1) == 0)
    def _():
        c_ref[...] = jnp.zeros_like(c_ref)
        n_ref[...] = jnp.zeros_like(n_ref)
        m_ref[...] = jnp.zeros_like(m_ref)

    row = lax.broadcasted_iota(jnp.int32, (L, L), 0)
    col = lax.broadcasted_iota(jnp.int32, (L, L), 1)
    causal = col <= row
    lower = causal.astype(bf16)
    upper = (row <= col).astype(bf16)

    g = g_ref[...]
    g1, g2, g3 = _split3(g)
    b_cols = (_dot(lower, g1) + _dot(lower, g2)) + _dot(lower, g3)
    gt = g.T[0:SUBLANES, :]
    t1, t2, t3 = _split3(gt)
    b_rows = (_dot(t1, upper) + _dot(t2, upper)) + _dot(t3, upper)

    for h in range(MLSTM_HEADS):
        hs = slice(h * Dh, (h + 1) * Dh)
        i_col = g[:, h:h + 1]
        b_col = b_cols[:, MLSTM_HEADS + h:MLSTM_HEADS + h + 1]
        a_row = gt[h:h + 1, :] - b_rows[MLSTM_HEADS + h:MLSTM_HEADS + h + 1, :]
        m_prev = m_ref[h][0:1, 0:1]
        c_prev = c_ref[h]
        n_prev = n_ref[h]

        log_intra = jnp.where(causal, b_col + a_row, -jnp.inf)
        log_inter = b_col + m_prev
        m_t = jnp.maximum(log_inter, jnp.max(log_intra, axis=1, keepdims=True))
        d_intra = jnp.exp(log_intra - m_t)
        d_inter = jnp.exp(log_inter - m_t)

        qh = q_ref[:, hs] * jnp.asarray(Dh ** -0.5, bf16)
        kh = k_ref[:, hs]
        vh = v_ref[:, hs]
        s = _dot_nt(qh, kh) * d_intra
        num = _dot(s.astype(bf16), vh) + d_inter * _dot(qh, c_prev.astype(bf16))
        qn = jnp.sum(qh.astype(f32) * n_prev, axis=1, keepdims=True)
        den = jnp.sum(s, axis=1, keepdims=True) + d_inter * qn
        hh = num / jnp.maximum(jnp.abs(den), jnp.exp(-m_t))

        y = hh * lax.rsqrt(jnp.mean(hh * hh, axis=1, keepdims=True) + EPS) * nw_ref[:, hs]
        y = y * jax.nn.sigmoid(o_ref[:, hs].astype(f32))
        out_ref[:, hs] = y.astype(bf16)

        b_last = b_col[L - 1:L, :]
        log_w = b_last - b_col + i_col
        m_new = jnp.maximum(b_last + m_prev, jnp.max(log_w, axis=0, keepdims=True))
        w = jnp.exp(log_w - m_new)
        decay = jnp.exp(b_last + m_prev - m_new)
        kw = kh.astype(f32) * w
        c_ref[h] = decay * c_prev + _dot_tn(kw.astype(bf16), vh)
        n_ref[h] = decay * n_prev + jnp.sum(kw, axis=0, keepdims=True)
        m_ref[h] = jnp.broadcast_to(m_new, m_ref.shape[1:])


def _mlstm(p, g, norm_w, batch, seq):
    M, D = p.shape[1], p.shape[2]
    L = MLSTM_CHUNK
    nc = seq // L

    def pspec(idx):
        return pl.BlockSpec((None, L, D), lambda b, c: (idx, b * nc + c, 0))

    return pl.pallas_call(
        _mlstm_kernel,
        grid=(batch, nc),
        in_specs=[pspec(0), pspec(1), pspec(2), pspec(3),
                  pl.BlockSpec((L, GATE_COLS), lambda b, c: (b * nc + c, 0)),
                  pl.BlockSpec((1, D), lambda b, c: (0, 0))],
        out_specs=pl.BlockSpec((L, D), lambda b, c: (b * nc + c, 0)),
        out_shape=jax.ShapeDtypeStruct((M, D), bf16),
        scratch_shapes=[pltpu.VMEM((MLSTM_HEADS, MLSTM_HEAD_DIM, MLSTM_HEAD_DIM), f32),
                        pltpu.VMEM((MLSTM_HEADS, 1, MLSTM_HEAD_DIM), f32),
                        pltpu.VMEM((MLSTM_HEADS, SUBLANES, LANES), f32)],
        compiler_params=pltpu.CompilerParams(
            dimension_semantics=("arbitrary", "arbitrary"), vmem_limit_bytes=VMEM_LIMIT),
        name="mlstm",
    )(p, p, p, p, g, norm_w)


def _fox_kernel(q_ref, k_ref, v_ref, f_ref, qw_ref, kw_ref, out_ref,
                qa_ref, ka_ref, vt_ref, acc_ref, m_ref):
    T = FOX_TILE
    Dh = FOX_HEAD_DIM
    S = q_ref.shape[0]
    pair = pl.program_id(1)

    r = lax.broadcasted_iota(jnp.int32, (LANES, LANES), 0) // Dh
    c = lax.broadcasted_iota(jnp.int32, (LANES, LANES), 1) // Dh
    same_head = (r == c).astype(bf16)

    def head_rms(a, w):
        ms = _dot((a * a).astype(bf16), same_head) * (1.0 / Dh)
        return a * lax.rsqrt(ms + EPS) * w

    qn = head_rms(q_ref[...].astype(f32), qw_ref[...]) * (Dh ** -0.5 * LOG2E)
    kn = head_rms(k_ref[...].astype(f32), kw_ref[...])

    f1, f2, f3 = _split3(f_ref[...] * (-LOG2E))
    pieces = jnp.concatenate([f1, f2, f3], axis=1)
    rr = lax.broadcasted_iota(jnp.int32, (3 * LANES, LANES), 0)
    cc = lax.broadcasted_iota(jnp.int32, (3 * LANES, LANES), 1)
    piece = rr // LANES
    head = rr % LANES - 2 * MLSTM_HEADS - pair * FOX_PAIR
    place = ((head >= 0) & (head < FOX_PAIR) & (cc == (1 - head) * Dh + piece)).astype(bf16)
    bias = _dot(pieces, place)

    lane = lax.broadcasted_iota(jnp.int32, (S, LANES), 1)
    for hh in range(FOX_PAIR):
        own = lane // Dh == hh
        b0 = (1 - hh) * Dh
        ones = jnp.where((lane >= b0) & (lane < b0 + 3), 1.0, 0.0)
        qa_ref[hh] = jnp.where(own, qn, ones).astype(bf16)
        ka_ref[hh] = jnp.where(own, kn, bias).astype(bf16)
    v_t = v_ref[...].astype(f32).T
    ones_rows = jnp.where(lax.broadcasted_iota(jnp.int32, (FOX_VROWS - Dh, S), 0) == 0, 1.0, 0.0)
    for hh in range(FOX_PAIR):
        vt_ref[hh] = jnp.concatenate([v_t[hh * Dh:(hh + 1) * Dh, :], ones_rows], axis=0).astype(bf16)

    nt = S // T
    items = [(j, i, hh) for j in range(nt) for i in range(j, nt) for hh in range(FOX_PAIR)]

    def logits(item):
        j, i, hh = item
        s = _dot_nt(ka_ref[hh, j * T:(j + 1) * T, :], qa_ref[hh, i * T:(i + 1) * T, :])
        if i == j:
            kk = lax.broadcasted_iota(jnp.int32, (T, T), 0)
            qq = lax.broadcasted_iota(jnp.int32, (T, T), 1)
            s = jnp.where(kk <= qq, s, -jnp.inf)
        return s

    def accumulate(item, s):
        j, i, hh = item
        cols = slice(i * T, (i + 1) * T)
        st = slice(hh * nt + i, hh * nt + i + 1)
        m_cur = jnp.max(s, axis=0, keepdims=True)
        if j == 0:
            m_new = m_cur
        else:
            m_old = m_ref[st, :]
            m_new = jnp.maximum(m_old, m_cur)
            alpha = jnp.exp2(m_old - m_new)
        p = jnp.exp2((s - m_new).astype(bf16))
        pv = _dot(vt_ref[hh, :, j * T:(j + 1) * T], p)[0:FOX_AROWS, :]
        if j == 0:
            acc_ref[hh, :, cols] = pv
        else:
            acc_ref[hh, :, cols] = alpha * acc_ref[hh, :, cols] + pv
        m_ref[st, :] = m_new

    pending = []
    for n in range(len(items) + FOX_LOOKAHEAD):
        if n < len(items):
            pending.append((items[n], logits(items[n])))
        if n >= FOX_LOOKAHEAD:
            accumulate(*pending.pop(0))

    for i in range(nt):
        cols = slice(i * T, (i + 1) * T)
        o_t = jnp.concatenate(
            [acc_ref[hh, 0:Dh, cols] / acc_ref[hh, Dh:Dh + 1, cols] for hh in range(FOX_PAIR)], axis=0)
        out_ref[cols, :] = o_t.T.astype(bf16)


def _fox(p, fcum, q_norm_w, k_norm_w, batch, seq):
    M, D = p.shape[1], p.shape[2]
    T = FOX_TILE
    nt = seq // T
    npairs = FOX_HEADS // FOX_PAIR

    def pspec(idx):
        return pl.BlockSpec((None, seq, LANES), lambda b, pr: (idx, b, pr))

    return pl.pallas_call(
        _fox_kernel,
        grid=(batch, npairs),
        in_specs=[pspec(4), pspec(5), pspec(6),
                  pl.BlockSpec((seq, GATE_COLS), lambda b, pr: (b, 0)),
                  pl.BlockSpec((1, LANES), lambda b, pr: (0, 0)),
                  pl.BlockSpec((1, LANES), lambda b, pr: (0, 0))],
        out_specs=pl.BlockSpec((seq, LANES), lambda b, pr: (b, pr)),
        out_shape=jax.ShapeDtypeStruct((M, D), bf16),
        scratch_shapes=[pltpu.VMEM((FOX_PAIR, seq, LANES), bf16),
                        pltpu.VMEM((FOX_PAIR, seq, LANES), bf16),
                        pltpu.VMEM((FOX_PAIR, FOX_VROWS, seq), bf16),
                        pltpu.VMEM((FOX_PAIR, FOX_AROWS, seq), f32),
                        pltpu.VMEM((FOX_PAIR * nt, T), f32)],
        compiler_params=pltpu.CompilerParams(
            dimension_semantics=("arbitrary", "arbitrary"), vmem_limit_bytes=VMEM_LIMIT),
        name="fox",
    )(p, p, p, fcum, q_norm_w, k_norm_w)


def _merge_kernel(x_ref, ada_ref, hm_ref, hf_ref, ga_ref, gb_ref, wa_ref, wb_ref, wo_ref, out_ref):
    ya = _dot(hm_ref[...], wa_ref[...])
    yb = _dot(hf_ref[...], wb_ref[...])
    merged = (jax.nn.sigmoid(ga_ref[...].astype(f32)) * ya
              + jax.nn.sigmoid(gb_ref[...].astype(f32)) * yb)
    y = _dot(merged.astype(bf16), wo_ref[...])
    out_ref[...] = x_ref[...] + ada_ref[2:3, :] * y


def _merge(x2, ada3, hm, hf, p, w_a, w_b, w_o, seq, tm):
    M, D = x2.shape
    tiles_per_seq = seq // tm
    row = pl.BlockSpec((tm, D), lambda m: (m, 0))
    wspec = pl.BlockSpec((D, D), lambda m: (0, 0), pipeline_mode=pl.Buffered(1))
    return pl.pallas_call(
        _merge_kernel,
        grid=(M // tm,),
        in_specs=[row,
                  pl.BlockSpec((None, 6, D), lambda m: (m // tiles_per_seq, 0, 0)),
                  row, row,
                  pl.BlockSpec((None, tm, D), lambda m: (7, m, 0)),
                  pl.BlockSpec((None, tm, D), lambda m: (8, m, 0)),
                  wspec, wspec, wspec],
        out_specs=row,
        out_shape=jax.ShapeDtypeStruct((M, D), f32),
        compiler_params=pltpu.CompilerParams(
            dimension_semantics=("arbitrary",), vmem_limit_bytes=VMEM_LIMIT),
        name="merge",
    )(x2, ada3, hm, hf, p, p, w_a, w_b, w_o)


def _ffn_kernel(x_ref, ada_ref, nw_ref, wup_ref, cw_ref, cb_ref, wdown_ref, out_ref,
                h_ref, ubuf_ref, tail_ref, acc_ref, *, tiles_per_seq, tf):
    tm = x_ref.shape[0]
    H = SUBLANES
    first = (pl.program_id(0) % tiles_per_seq) == 0
    x = x_ref[...]
    h_ref[...] = _rms_modulate(x, nw_ref[...], ada_ref[3:4, :], ada_ref[4:5, :]).astype(bf16)

    def conv_cols(c0):
        cols = slice(c0, c0 + tf)
        u = _dot(h_ref[...], wup_ref[:, cols])
        ubuf_ref[H:H + tm, :] = u

        @pl.when(first)
        def _():
            ubuf_ref[0:H, :] = jnp.zeros((H, tf), f32)

        @pl.when(jnp.logical_not(first))
        def _():
            ubuf_ref[0:H, :] = tail_ref[:, cols]

        tail_ref[:, cols] = u[tm - H:tm, :]
        return (u * cw_ref[2:3, cols] + ubuf_ref[H - 1:H - 1 + tm, :] * cw_ref[1:2, cols]
                + ubuf_ref[H - 2:H - 2 + tm, :] * cw_ref[0:1, cols] + cb_ref[:, cols])

    for f in range(FFN_DIM // tf):
        ug = conv_cols(f * tf)
        uv = conv_cols(FFN_DIM + f * tf)
        a = ((ug * jax.nn.sigmoid(ug)) * uv).astype(bf16)
        y = _dot(a, wdown_ref[f * tf:(f + 1) * tf, :])
        if f == 0:
            acc_ref[...] = y
        else:
            acc_ref[...] += y
    out_ref[...] = x + ada_ref[5:6, :] * acc_ref[...]


def _ffn(x2, ada3, norm_w, w_up, conv_w, conv_b, w_down, seq, tm, tf):
    M, D = x2.shape
    F2 = w_up.shape[1]
    tiles_per_seq = seq // tm
    row = pl.BlockSpec((tm, D), lambda m: (m, 0))

    def const(shape):
        return pl.BlockSpec(shape, lambda m: (0, 0), pipeline_mode=pl.Buffered(1))

    return pl.pallas_call(
        functools.partial(_ffn_kernel, tiles_per_seq=tiles_per_seq, tf=tf),
        grid=(M // tm,),
        in_specs=[row,
                  pl.BlockSpec((None, 6, D), lambda m: (m // tiles_per_seq, 0, 0)),
                  const((1, D)), const((D, F2)), const((CONV_WIDTH, F2)), const((1, F2)),
                  const((F2 // 2, D))],
        out_specs=row,
        out_shape=jax.ShapeDtypeStruct((M, D), f32),
        scratch_shapes=[pltpu.VMEM((tm, D), bf16),
                        pltpu.VMEM((tm + SUBLANES, tf), f32),
                        pltpu.VMEM((SUBLANES, F2), f32),
                        pltpu.VMEM((tm, D), f32)],
        compiler_params=pltpu.CompilerParams(
            dimension_semantics=("arbitrary",), vmem_limit_bytes=VMEM_LIMIT),
        name="ffn",
    )(x2, ada3, norm_w, w_up, conv_w, conv_b, w_down)


def _layer(x2, c, batch, seq, w_ada, b_ada, norm1_w, w_in, b_mlstm_i, b_mlstm_f, mlstm_norm_w,
           b_fox_f, fox_q_norm_w, fox_k_norm_w, w_branch_mlstm, w_branch_fox, w_out,
           norm2_w, w_up, conv_w, conv_b, w_down):
    D = D_MODEL
    ada3 = _ada(c, w_ada, b_ada.reshape(1, -1)).reshape(batch, 6, D)

    o_mi = 4 * D
    o_mf = o_mi + MLSTM_HEADS
    o_fq = o_mf + MLSTM_HEADS
    o_ff = o_fq + 3 * D
    o_ga = o_ff + FOX_HEADS
    w_big = jnp.concatenate([w_in[:, :o_mi], w_in[:, o_fq:o_ff], w_in[:, o_ga:]], axis=1).astype(bf16)
    n_gate = 2 * MLSTM_HEADS + FOX_HEADS
    w_gate = jnp.concatenate([w_in[:, o_mi:o_fq], w_in[:, o_ff:o_ga],
                              jnp.zeros((D, GATE_COLS - n_gate), f32)], axis=1).astype(bf16)
    b_gate = jnp.concatenate([b_mlstm_i, b_mlstm_f, b_fox_f,
                              jnp.zeros((GATE_COLS - n_gate,), f32)]).reshape(1, GATE_COLS)

    p, g = _in_proj(x2, ada3, norm1_w.reshape(1, D), w_big, w_gate, b_gate, seq, tm=1024)
    fcum = _fox_cum(g, batch, seq)
    hm = _mlstm(p, g, mlstm_norm_w.reshape(1, D), batch, seq)
    hf = _fox(p, fcum, jnp.tile(fox_q_norm_w, FOX_PAIR).reshape(1, LANES),
              jnp.tile(fox_k_norm_w, FOX_PAIR).reshape(1, LANES), batch, seq)
    x2 = _merge(x2, ada3, hm, hf, p, w_branch_mlstm.astype(bf16), w_branch_fox.astype(bf16),
                w_out.astype(bf16), seq, tm=512)
    return _ffn(x2, ada3, norm2_w.reshape(1, D), w_up.astype(bf16), conv_w, conv_b.reshape(1, -1),
                w_down.astype(bf16), seq, tm=512, tf=256)


def kernel(x, c, w_ada, b_ada, norm1_w, w_in, b_mlstm_i, b_mlstm_f, mlstm_norm_w, b_fox_f,
           fox_q_norm_w, fox_k_norm_w, w_branch_mlstm, w_branch_fox, w_out, norm2_w, w_up,
           conv_w, conv_b, w_down):
    batch, seq, D = x.shape
    x2 = x.reshape(batch * seq, D)
    for l in range(w_ada.shape[0]):
        x2 = _layer(x2, c, batch, seq, w_ada[l], b_ada[l], norm1_w[l], w_in[l], b_mlstm_i[l],
                    b_mlstm_f[l], mlstm_norm_w[l], b_fox_f[l], fox_q_norm_w[l], fox_k_norm_w[l],
                    w_branch_mlstm[l], w_branch_fox[l], w_out[l], norm2_w[l], w_up[l],
                    conv_w[l], conv_b[l], w_down[l])
    return x2.reshape(batch, seq, D)
```

```python
import functools

import jax
import jax.numpy as jnp
from jax import lax
from jax.experimental import pallas as pl
from jax.experimental.pallas import tpu as pltpu

D_MODEL = 1024
MLSTM_HEADS = 4
MLSTM_HEAD_DIM = 256
FOX_HEADS = 16
FOX_HEAD_DIM = 64
FFN_DIM = 2816
CONV_WIDTH = 3
EPS = 1e-6

LANES = 128
SUBLANES = 8
N_BIG = 9
GATE_COLS = LANES
MLSTM_CHUNK = 256
FOX_TILE = 256
FOX_PAIR = LANES // FOX_HEAD_DIM
FOX_VROWS = FOX_HEAD_DIM + 16
FOX_AROWS = FOX_HEAD_DIM + SUBLANES
FOX_LOOKAHEAD = 4
FFN_LOOKAHEAD = 2
FFN_SLOTS = FFN_LOOKAHEAD + 1
VMEM_LIMIT = 56 * 1024 * 1024
LOG2E = 1.4426950408889634

f32 = jnp.float32
bf16 = jnp.bfloat16


def _dot(a, b):
    return jnp.dot(a, b, preferred_element_type=f32)


def _dot_nt(a, b):
    return lax.dot_general(a, b, (((1,), (1,)), ((), ())), preferred_element_type=f32)


def _dot_tn(a, b):
    return lax.dot_general(a, b, (((0,), (0,)), ((), ())), preferred_element_type=f32)


def _split3(a):
    a1 = a.astype(bf16)
    r1 = a - a1.astype(f32)
    a2 = r1.astype(bf16)
    a3 = (r1 - a2.astype(f32)).astype(bf16)
    return a1, a2, a3


def _log_sigmoid(x):
    return jnp.minimum(x, 0.0) - jnp.log1p(jnp.exp(-jnp.abs(x)))


def _rms_modulate(x, norm_w, shift, scale):
    y = x * lax.rsqrt(jnp.mean(x * x, axis=-1, keepdims=True) + EPS)
    return (y * norm_w) * (1.0 + scale) + shift


def _ada_kernel(c_ref, w_ref, b_ref, out_ref):
    c = c_ref[...]
    a = (c * jax.nn.sigmoid(c)).astype(bf16)
    out_ref[...] = _dot(a, w_ref[...].astype(bf16)) + b_ref[...]


def _ada(c, w_ada, b_ada):
    B, D = c.shape
    n = w_ada.shape[1] // D
    return pl.pallas_call(
        _ada_kernel,
        grid=(n,),
        in_specs=[pl.BlockSpec((B, D), lambda j: (0, 0)),
                  pl.BlockSpec((D, D), lambda j: (0, j)),
                  pl.BlockSpec((1, D), lambda j: (0, j))],
        out_specs=pl.BlockSpec((B, D), lambda j: (0, j)),
        out_shape=jax.ShapeDtypeStruct((B, n * D), f32),
        name="ada",
    )(c, w_ada, b_ada)


def _in_proj_kernel(x_ref, ada_ref, nw_ref, w_ref, wg_ref, bg_ref, p_ref, g_ref, h_ref):
    D = x_ref.shape[1]
    h = _rms_modulate(x_ref[...], nw_ref[...], ada_ref[0:1, :], ada_ref[1:2, :]).astype(bf16)
    h_ref[...] = h
    for j in range(N_BIG):
        cols = slice(j * D, (j + 1) * D)
        p_ref[:, cols] = _dot(h_ref[...], w_ref[:, cols]).astype(bf16)
    g = _dot(h_ref[...], wg_ref[...]) + bg_ref[...]
    col = lax.broadcasted_iota(jnp.int32, g.shape, 1)
    g_ref[...] = jnp.where(col < MLSTM_HEADS, g, _log_sigmoid(g))


def _in_proj(x2, ada3, norm_w, w_big, w_gate, b_gate, seq, tm):
    M, D = x2.shape
    tiles_per_seq = seq // tm

    def const(shape):
        return pl.BlockSpec(shape, lambda m: (0, 0), pipeline_mode=pl.Buffered(1))

    return pl.pallas_call(
        _in_proj_kernel,
        grid=(M // tm,),
        in_specs=[pl.BlockSpec((tm, D), lambda m: (m, 0)),
                  pl.BlockSpec((None, 6, D), lambda m: (m // tiles_per_seq, 0, 0)),
                  const((1, D)), const((D, N_BIG * D)), const((D, GATE_COLS)), const((1, GATE_COLS))],
        out_specs=[pl.BlockSpec((tm, N_BIG * D), lambda m: (m, 0)),
                   pl.BlockSpec((tm, GATE_COLS), lambda m: (m, 0))],
        out_shape=[jax.ShapeDtypeStruct((M, N_BIG * D), bf16),
                   jax.ShapeDtypeStruct((M, GATE_COLS), f32)],
        scratch_shapes=[pltpu.VMEM((tm, D), bf16)],
        compiler_params=pltpu.CompilerParams(
            dimension_semantics=("arbitrary",), vmem_limit_bytes=VMEM_LIMIT),
        name="in_proj",
    )(x2, ada3, norm_w, w_big, w_gate, b_gate)


def _fox_cum_kernel(g_ref, f_ref):
    S = g_ref.shape[0]
    T = FOX_TILE
    row = lax.broadcasted_iota(jnp.int32, (T, T), 0)
    col = lax.broadcasted_iota(jnp.int32, (T, T), 1)
    lower = (col <= row).astype(bf16)
    carry = jnp.zeros((1, GATE_COLS), f32)
    for i in range(S // T):
        a1, a2, a3 = _split3(g_ref[i * T:(i + 1) * T, :])
        cs = (_dot(lower, a1) + _dot(lower, a2)) + _dot(lower, a3) + carry
        f_ref[i * T:(i + 1) * T, :] = cs
        carry = cs[T - 1:T, :]


def _fox_cum(g, batch, seq):
    return pl.pallas_call(
        _fox_cum_kernel,
        grid=(batch,),
        in_specs=[pl.BlockSpec((seq, GATE_COLS), lambda b: (b, 0))],
        out_specs=pl.BlockSpec((seq, GATE_COLS), lambda b: (b, 0)),
        out_shape=jax.ShapeDtypeStruct((batch * seq, GATE_COLS), f32),
        name="fox_cum",
    )(g)


def _mlstm_kernel(q_ref, k_ref, v_ref, o_ref, g_ref, nw_ref, out_ref, c_ref, m_ref):
    L = MLSTM_CHUNK
    Dh = MLSTM_HEAD_DIM
    heads = range(MLSTM_HEADS)

    @pl.when(pl.program_id(1) == 0)
    def _():
        c_ref[...] = jnp.zeros_like(c_ref)
        m_ref[...] = jnp.zeros_like(m_ref)

    row = lax.broadcasted_iota(jnp.int32, (L, L), 0)
    col = lax.broadcasted_iota(jnp.int32, (L, L), 1)
    causal = col <= row
    lower = causal.astype(bf16)
    upper = (row <= col).astype(bf16)

    g = g_ref[...]
    g1, g2, g3 = _split3(g)
    b_cols = (_dot(lower, g1) + _dot(lower, g2)) + _dot(lower, g3)
    gt = g.T[0:SUBLANES, :]
    t1, t2, t3 = _split3(gt)
    b_rows = (_dot(t1, upper) + _dot(t2, upper)) + _dot(t3, upper)

    def hs(h):
        return slice(h * Dh, (h + 1) * Dh)

    ones_col = jnp.where(lax.broadcasted_iota(jnp.int32, (L, LANES), 1) == 0, 1.0, 0.0).astype(bf16)
    qs = [q_ref[:, hs(h)] * jnp.asarray(Dh ** -0.5, bf16) for h in heads]
    vs = [jnp.concatenate([v_ref[:, hs(h)], ones_col], axis=1) for h in heads]

    qk = [_dot_nt(qs[h], k_ref[:, hs(h)]) for h in heads]
    qc = [_dot(qs[h], c_ref[h].astype(bf16)) for h in heads]

    d_intra, d_inter, m_t, decay = [], [], [], []
    for h in heads:
        i_col = g[:, h:h + 1]
        b_col = b_cols[:, MLSTM_HEADS + h:MLSTM_HEADS + h + 1]
        a_row = gt[h:h + 1, :] - b_rows[MLSTM_HEADS + h:MLSTM_HEADS + h + 1, :]
        m_prev = m_ref[h][0:1, 0:1]

        log_intra = jnp.where(causal, b_col + a_row, -jnp.inf)
        log_inter = b_col + m_prev
        m_h = jnp.maximum(log_inter, jnp.max(log_intra, axis=1, keepdims=True))
        m_t.append(m_h)
        d_intra.append(jnp.exp(log_intra - m_h))
        d_inter.append(jnp.exp(log_inter - m_h))

        b_last = b_col[L - 1:L, :]
        log_w = b_last - b_col + i_col
        m_new = jnp.maximum(b_last + m_prev, jnp.max(log_w, axis=0, keepdims=True))
        w = jnp.exp(log_w - m_new)
        decay.append(jnp.exp(b_last + m_prev - m_new))
        kw = (k_ref[:, hs(h)].astype(f32) * w).astype(bf16)
        c_ref[h] = decay[h] * c_ref[h] + _dot_tn(kw, vs[h])
        m_ref[h] = jnp.broadcast_to(m_new, m_ref.shape[1:])

    for h in heads:
        s = (qk[h] * d_intra[h]).astype(bf16)
        tot = _dot(s, vs[h]) + d_inter[h] * qc[h]
        den = tot[:, Dh:Dh + 1]
        hh = tot[:, 0:Dh] / jnp.maximum(jnp.abs(den), jnp.exp(-m_t[h]))
        y = hh * lax.rsqrt(jnp.mean(hh * hh, axis=1, keepdims=True) + EPS) * nw_ref[:, hs(h)]
        y = y * jax.nn.sigmoid(o_ref[:, hs(h)].astype(f32))
        out_ref[:, hs(h)] = y.astype(bf16)


def _mlstm(p, g, norm_w, batch, seq):
    M, D = p.shape[0], D_MODEL
    L = MLSTM_CHUNK
    nc = seq // L

    def pspec(idx):
        return pl.BlockSpec((L, D), lambda b, c: (b * nc + c, idx))

    return pl.pallas_call(
        _mlstm_kernel,
        grid=(batch, nc),
        in_specs=[pspec(0), pspec(1), pspec(2), pspec(3),
                  pl.BlockSpec((L, GATE_COLS), lambda b, c: (b * nc + c, 0)),
                  pl.BlockSpec((1, D), lambda b, c: (0, 0))],
        out_specs=pl.BlockSpec((L, D), lambda b, c: (b * nc + c, 0)),
        out_shape=jax.ShapeDtypeStruct((M, D), bf16),
        scratch_shapes=[pltpu.VMEM((MLSTM_HEADS, MLSTM_HEAD_DIM, MLSTM_HEAD_DIM + LANES), f32),
                        pltpu.VMEM((MLSTM_HEADS, SUBLANES, LANES), f32)],
        compiler_params=pltpu.CompilerParams(
            dimension_semantics=("arbitrary", "arbitrary"), vmem_limit_bytes=VMEM_LIMIT),
        name="mlstm",
    )(p, p, p, p, g, norm_w)


def _fox_kernel(q_ref, k_ref, v_ref, f_ref, qw_ref, kw_ref, out_ref,
                qa_ref, ka_ref, vt_ref, acc_ref, m_ref):
    T = FOX_TILE
    Dh = FOX_HEAD_DIM
    S = q_ref.shape[0]
    pair = pl.program_id(1)

    r = lax.broadcasted_iota(jnp.int32, (LANES, LANES), 0) // Dh
    c = lax.broadcasted_iota(jnp.int32, (LANES, LANES), 1) // Dh
    same_head = (r == c).astype(bf16)

    def head_rms(a, w):
        ms = _dot((a * a).astype(bf16), same_head) * (1.0 / Dh)
        return a * lax.rsqrt(ms + EPS) * w

    qn = head_rms(q_ref[...].astype(f32), qw_ref[...]) * (Dh ** -0.5 * LOG2E)
    kn = head_rms(k_ref[...].astype(f32), kw_ref[...])

    f1, f2, f3 = _split3(f_ref[...] * (-LOG2E))
    pieces = jnp.concatenate([f1, f2, f3], axis=1)
    rr = lax.broadcasted_iota(jnp.int32, (3 * LANES, LANES), 0)
    cc = lax.broadcasted_iota(jnp.int32, (3 * LANES, LANES), 1)
    piece = rr // LANES
    head = rr % LANES - 2 * MLSTM_HEADS - pair * FOX_PAIR
    place = ((head >= 0) & (head < FOX_PAIR) & (cc == (1 - head) * Dh + piece)).astype(bf16)
    bias = _dot(pieces, place)

    lane = lax.broadcasted_iota(jnp.int32, (S, LANES), 1)
    for hh in range(FOX_PAIR):
        own = lane // Dh == hh
        b0 = (1 - hh) * Dh
        ones = jnp.where((lane >= b0) & (lane < b0 + 3), 1.0, 0.0)
        qa_ref[hh] = jnp.where(own, qn, ones).astype(bf16)
        ka_ref[hh] = jnp.where(own, kn, bias).astype(bf16)
    v_t = v_ref[...].astype(f32).T
    ones_rows = jnp.where(lax.broadcasted_iota(jnp.int32, (FOX_VROWS - Dh, S), 0) == 0, 1.0, 0.0)
    for hh in range(FOX_PAIR):
        vt_ref[hh] = jnp.concatenate([v_t[hh * Dh:(hh + 1) * Dh, :], ones_rows], axis=0).astype(bf16)

    nt = S // T
    items = [(j, i, hh) for j in range(nt) for i in range(j, nt) for hh in range(FOX_PAIR)]

    def logits(item):
        j, i, hh = item
        s = _dot_nt(ka_ref[hh, j * T:(j + 1) * T, :], qa_ref[hh, i * T:(i + 1) * T, :])
        if i == j:
            kk = lax.broadcasted_iota(jnp.int32, (T, T), 0)
            qq = lax.broadcasted_iota(jnp.int32, (T, T), 1)
            s = jnp.where(kk <= qq, s, -jnp.inf)
        return s

    def accumulate(item, s):
        j, i, hh = item
        cols = slice(i * T, (i + 1) * T)
        st = slice(hh * nt + i, hh * nt + i + 1)
        m_cur = jnp.max(s, axis=0, keepdims=True)
        if j == 0:
            m_new = m_cur
        else:
            m_old = m_ref[st, :]
            m_new = jnp.maximum(m_old, m_cur)
            alpha = jnp.exp2(m_old - m_new)
        p = jnp.exp2((s - m_new).astype(bf16))
        pv = _dot(vt_ref[hh, :, j * T:(j + 1) * T], p)[0:FOX_AROWS, :]
        if j == 0:
            acc_ref[hh, :, cols] = pv
        else:
            acc_ref[hh, :, cols] = alpha * acc_ref[hh, :, cols] + pv
        m_ref[st, :] = m_new

    pending = []
    for n in range(len(items) + FOX_LOOKAHEAD):
        if n < len(items):
            pending.append((items[n], logits(items[n])))
        if n >= FOX_LOOKAHEAD:
            accumulate(*pending.pop(0))

    for i in range(nt):
        cols = slice(i * T, (i + 1) * T)
        o_t = jnp.concatenate(
            [acc_ref[hh, 0:Dh, cols] / acc_ref[hh, Dh:Dh + 1, cols] for hh in range(FOX_PAIR)], axis=0)
        out_ref[cols, :] = o_t.T.astype(bf16)


def _fox(p, fcum, q_norm_w, k_norm_w, batch, seq):
    M, D = p.shape[0], D_MODEL
    T = FOX_TILE
    nt = seq // T
    npairs = FOX_HEADS // FOX_PAIR

    def pspec(idx):
        return pl.BlockSpec((seq, LANES), lambda b, pr: (b, idx * npairs + pr))

    return pl.pallas_call(
        _fox_kernel,
        grid=(batch, npairs),
        in_specs=[pspec(4), pspec(5), pspec(6),
                  pl.BlockSpec((seq, GATE_COLS), lambda b, pr: (b, 0)),
                  pl.BlockSpec((1, LANES), lambda b, pr: (0, 0)),
                  pl.BlockSpec((1, LANES), lambda b, pr: (0, 0))],
        out_specs=pl.BlockSpec((seq, LANES), lambda b, pr: (b, pr)),
        out_shape=jax.ShapeDtypeStruct((M, D), bf16),
        scratch_shapes=[pltpu.VMEM((FOX_PAIR, seq, LANES), bf16),
                        pltpu.VMEM((FOX_PAIR, seq, LANES), bf16),
                        pltpu.VMEM((FOX_PAIR, FOX_VROWS, seq), bf16),
                        pltpu.VMEM((FOX_PAIR, FOX_AROWS, seq), f32),
                        pltpu.VMEM((FOX_PAIR * nt, T), f32)],
        compiler_params=pltpu.CompilerParams(
            dimension_semantics=("arbitrary", "arbitrary"), vmem_limit_bytes=VMEM_LIMIT),
        name="fox",
    )(p, p, p, fcum, q_norm_w, k_norm_w)


def _merge_kernel(x_ref, ada_ref, hm_ref, hf_ref, ga_ref, gb_ref, wa_ref, wb_ref, wo_ref, out_ref):
    ya = _dot(hm_ref[...], wa_ref[...])
    yb = _dot(hf_ref[...], wb_ref[...])
    merged = (jax.nn.sigmoid(ga_ref[...].astype(f32)) * ya
              + jax.nn.sigmoid(gb_ref[...].astype(f32)) * yb)
    y = _dot(merged.astype(bf16), wo_ref[...])
    out_ref[...] = x_ref[...] + ada_ref[2:3, :] * y


def _merge(x2, ada3, hm, hf, p, w_a, w_b, w_o, seq, tm):
    M, D = x2.shape
    tiles_per_seq = seq // tm
    row = pl.BlockSpec((tm, D), lambda m: (m, 0))
    wspec = pl.BlockSpec((D, D), lambda m: (0, 0), pipeline_mode=pl.Buffered(1))
    return pl.pallas_call(
        _merge_kernel,
        grid=(M // tm,),
        in_specs=[row,
                  pl.BlockSpec((None, 6, D), lambda m: (m // tiles_per_seq, 0, 0)),
                  row, row,
                  pl.BlockSpec((tm, D), lambda m: (m, 7)),
                  pl.BlockSpec((tm, D), lambda m: (m, 8)),
                  wspec, wspec, wspec],
        out_specs=row,
        out_shape=jax.ShapeDtypeStruct((M, D), f32),
        compiler_params=pltpu.CompilerParams(
            dimension_semantics=("arbitrary",), vmem_limit_bytes=VMEM_LIMIT),
        name="merge",
    )(x2, ada3, hm, hf, p, p, w_a, w_b, w_o)


def _ffn_kernel(x_ref, ada_ref, nw_ref, wup_ref, cw_ref, cb_ref, wdown_ref, out_ref,
                h_ref, ubuf_ref, tail_ref, acc_ref, *, tiles_per_seq, tf):
    tm = x_ref.shape[0]
    H = SUBLANES
    nf = FFN_DIM // tf

    @pl.when(pl.program_id(0) == 0)
    def _():
        tail_ref[...] = jnp.zeros_like(tail_ref)

    first = (pl.program_id(0) % tiles_per_seq) == 0
    h_ref[...] = _rms_modulate(x_ref[...], nw_ref[...], ada_ref[3:4, :], ada_ref[4:5, :]).astype(bf16)

    def up(f):
        for part in range(2):
            cols = slice(part * FFN_DIM + f * tf, part * FFN_DIM + (f + 1) * tf)
            buf = ubuf_ref.at[f % FFN_SLOTS, part]
            u = _dot(h_ref[...], wup_ref[:, cols])
            buf[H:H + tm, :] = u
            buf[0:H, :] = jnp.where(first, 0.0, tail_ref[:, cols])
            tail_ref[:, cols] = u[tm - H:tm, :]

    def conv(f, part):
        cols = slice(part * FFN_DIM + f * tf, part * FFN_DIM + (f + 1) * tf)
        buf = ubuf_ref.at[f % FFN_SLOTS, part]
        return (buf[H:H + tm, :] * cw_ref[2:3, cols] + buf[H - 1:H - 1 + tm, :] * cw_ref[1:2, cols]
                + buf[H - 2:H - 2 + tm, :] * cw_ref[0:1, cols] + cb_ref[:, cols])

    def down(f):
        ug = conv(f, 0)
        a = ((ug * jax.nn.sigmoid(ug)) * conv(f, 1)).astype(bf16)
        y = _dot(a, wdown_ref[f * tf:(f + 1) * tf, :])
        if f == 0:
            acc_ref[...] = y
        else:
            acc_ref[...] += y

    for n in range(nf + FFN_LOOKAHEAD):
        if n < nf:
            up(n)
        if n >= FFN_LOOKAHEAD:
            down(n - FFN_LOOKAHEAD)
    out_ref[...] = x_ref[...] + ada_ref[5:6, :] * acc_ref[...]


def _ffn(x2, ada3, norm_w, w_up, conv_w, conv_b, w_down, seq, tm, tf):
    M, D = x2.shape
    F2 = w_up.shape[1]
    tiles_per_seq = seq // tm
    row = pl.BlockSpec((tm, D), lambda m: (m, 0))

    def const(shape):
        return pl.BlockSpec(shape, lambda m: (0, 0), pipeline_mode=pl.Buffered(1))

    return pl.pallas_call(
        functools.partial(_ffn_kernel, tiles_per_seq=tiles_per_seq, tf=tf),
        grid=(M // tm,),
        in_specs=[row,
                  pl.BlockSpec((None, 6, D), lambda m: (m // tiles_per_seq, 0, 0)),
                  const((1, D)), const((D, F2)), const((CONV_WIDTH, F2)), const((1, F2)),
                  const((F2 // 2, D))],
        out_specs=row,
        out_shape=jax.ShapeDtypeStruct((M, D), f32),
        scratch_shapes=[pltpu.VMEM((tm, D), bf16),
                        pltpu.VMEM((FFN_SLOTS, 2, tm + SUBLANES, tf), f32),
                        pltpu.VMEM((SUBLANES, F2), f32),
                        pltpu.VMEM((tm, D), f32)],
        compiler_params=pltpu.CompilerParams(
            dimension_semantics=("arbitrary",), vmem_limit_bytes=VMEM_LIMIT),
        name="ffn",
    )(x2, ada3, norm_w, w_up, conv_w, conv_b, w_down)


def _layer(x2, c, batch, seq, w_ada, b_ada, norm1_w, w_in, b_mlstm_i, b_mlstm_f, mlstm_norm_w,
           b_fox_f, fox_q_norm_w, fox_k_norm_w, w_branch_mlstm, w_branch_fox, w_out,
           norm2_w, w_up, conv_w, conv_b, w_down):
    D = D_MODEL
    ada3 = _ada(c, w_ada, b_ada.reshape(1, -1)).reshape(batch, 6, D)

    o_mi = 4 * D
    o_mf = o_mi + MLSTM_HEADS
    o_fq = o_mf + MLSTM_HEADS
    o_ff = o_fq + 3 * D
    o_ga = o_ff + FOX_HEADS
    w_big = jnp.concatenate([w_in[:, :o_mi], w_in[:, o_fq:o_ff], w_in[:, o_ga:]], axis=1).astype(bf16)
    n_gate = 2 * MLSTM_HEADS + FOX_HEADS
    w_gate = jnp.concatenate([w_in[:, o_mi:o_fq], w_in[:, o_ff:o_ga],
                              jnp.zeros((D, GATE_COLS - n_gate), f32)], axis=1).astype(bf16)
    b_gate = jnp.concatenate([b_mlstm_i, b_mlstm_f, b_fox_f,
                              jnp.zeros((GATE_COLS - n_gate,), f32)]).reshape(1, GATE_COLS)

    p, g = _in_proj(x2, ada3, norm1_w.reshape(1, D), w_big, w_gate, b_gate, seq, tm=512)
    fcum = _fox_cum(g, batch, seq)
    hm = _mlstm(p, g, mlstm_norm_w.reshape(1, D), batch, seq)
    hf = _fox(p, fcum, jnp.tile(fox_q_norm_w, FOX_PAIR).reshape(1, LANES),
              jnp.tile(fox_k_norm_w, FOX_PAIR).reshape(1, LANES), batch, seq)
    x2 = _merge(x2, ada3, hm, hf, p, w_branch_mlstm.astype(bf16), w_branch_fox.astype(bf16),
                w_out.astype(bf16), seq, tm=512)
    return _ffn(x2, ada3, norm2_w.reshape(1, D), w_up.astype(bf16), conv_w, conv_b.reshape(1, -1),
                w_down.astype(bf16), seq, tm=512, tf=256)


def kernel(x, c, w_ada, b_ada, norm1_w, w_in, b_mlstm_i, b_mlstm_f, mlstm_norm_w, b_fox_f,
           fox_q_norm_w, fox_k_norm_w, w_branch_mlstm, w_branch_fox, w_out, norm2_w, w_up,
           conv_w, conv_b, w_down):
    batch, seq, D = x.shape
    x2 = x.reshape(batch * seq, D)
    for l in range(w_ada.shape[0]):
        x2 = _layer(x2, c, batch, seq, w_ada[l], b_ada[l], norm1_w[l], w_in[l], b_mlstm_i[l],
                    b_mlstm_f[l], mlstm_norm_w[l], b_fox_f[l], fox_q_norm_w[l], fox_k_norm_w[l],
                    w_branch_mlstm[l], w_branch_fox[l], w_out[l], norm2_w[l], w_up[l],
                    conv_w[l], conv_b[l], w_down[l])
    return x2.reshape(batch, seq, D)
```

```python
import functools

import jax
import jax.numpy as jnp
from jax import lax
from jax.experimental import pallas as pl
from jax.experimental.pallas import tpu as pltpu

D_MODEL = 1024
MLSTM_HEADS = 4
MLSTM_HEAD_DIM = 256
FOX_HEADS = 16
FOX_HEAD_DIM = 64
FFN_DIM = 2816
CONV_WIDTH = 3
EPS = 1e-6

LANES = 128
SUBLANES = 8
N_BIG = 9
GATE_COLS = LANES
MLSTM_CHUNK = 256
MLSTM_VROWS = MLSTM_HEAD_DIM + 16
FOX_TILE = 256
FOX_PAIR = LANES // FOX_HEAD_DIM
FOX_VROWS = FOX_HEAD_DIM + 16
FOX_AROWS = FOX_HEAD_DIM + SUBLANES
FOX_LOOKAHEAD = 5
FFN_LOOKAHEAD = 2
FFN_SLOTS = FFN_LOOKAHEAD + 1
VMEM_LIMIT = 56 * 1024 * 1024
LOG2E = 1.4426950408889634

f32 = jnp.float32
bf16 = jnp.bfloat16


def _dot(a, b):
    return jnp.dot(a, b, preferred_element_type=f32)


def _dot_nt(a, b):
    return lax.dot_general(a, b, (((1,), (1,)), ((), ())), preferred_element_type=f32)


def _dot_tn(a, b):
    return lax.dot_general(a, b, (((0,), (0,)), ((), ())), preferred_element_type=f32)


def _split3(a):
    a1 = a.astype(bf16)
    r1 = a - a1.astype(f32)
    a2 = r1.astype(bf16)
    a3 = (r1 - a2.astype(f32)).astype(bf16)
    return a1, a2, a3


def _log_sigmoid(x):
    return jnp.minimum(x, 0.0) - jnp.log1p(jnp.exp(-jnp.abs(x)))


def _rms_modulate(x, norm_w, shift, scale):
    y = x * lax.rsqrt(jnp.mean(x * x, axis=-1, keepdims=True) + EPS)
    return (y * norm_w) * (1.0 + scale) + shift


def _ada_kernel(c_ref, w_ref, b_ref, out_ref):
    c = c_ref[...]
    a = (c * jax.nn.sigmoid(c)).astype(bf16)
    out_ref[...] = _dot(a, w_ref[...].astype(bf16)) + b_ref[...]


def _ada(c, w_ada, b_ada):
    B, D = c.shape
    n = w_ada.shape[1] // D
    return pl.pallas_call(
        _ada_kernel,
        grid=(n,),
        in_specs=[pl.BlockSpec((B, D), lambda j: (0, 0)),
                  pl.BlockSpec((D, D), lambda j: (0, j)),
                  pl.BlockSpec((1, D), lambda j: (0, j))],
        out_specs=pl.BlockSpec((B, D), lambda j: (0, j)),
        out_shape=jax.ShapeDtypeStruct((B, n * D), f32),
        name="ada",
    )(c, w_ada, b_ada)


def _in_proj_kernel(x_ref, ada_ref, nw_ref, w_ref, wg_ref, bg_ref, p_ref, g_ref, h_ref):
    D = x_ref.shape[1]
    h = _rms_modulate(x_ref[...], nw_ref[...], ada_ref[0:1, :], ada_ref[1:2, :]).astype(bf16)
    h_ref[...] = h
    for j in range(N_BIG):
        cols = slice(j * D, (j + 1) * D)
        p_ref[:, cols] = _dot(h_ref[...], w_ref[:, cols]).astype(bf16)
    g = _dot(h_ref[...], wg_ref[...]) + bg_ref[...]
    col = lax.broadcasted_iota(jnp.int32, g.shape, 1)
    g_ref[...] = jnp.where(col < MLSTM_HEADS, g, _log_sigmoid(g))


def _in_proj(x2, ada3, norm_w, w_big, w_gate, b_gate, seq, tm):
    M, D = x2.shape
    tiles_per_seq = seq // tm

    def const(shape):
        return pl.BlockSpec(shape, lambda m: (0, 0), pipeline_mode=pl.Buffered(1))

    return pl.pallas_call(
        _in_proj_kernel,
        grid=(M // tm,),
        in_specs=[pl.BlockSpec((tm, D), lambda m: (m, 0)),
                  pl.BlockSpec((None, 6, D), lambda m: (m // tiles_per_seq, 0, 0)),
                  const((1, D)), const((D, N_BIG * D)), const((D, GATE_COLS)), const((1, GATE_COLS))],
        out_specs=[pl.BlockSpec((tm, N_BIG * D), lambda m: (m, 0)),
                   pl.BlockSpec((tm, GATE_COLS), lambda m: (m, 0))],
        out_shape=[jax.ShapeDtypeStruct((M, N_BIG * D), bf16),
                   jax.ShapeDtypeStruct((M, GATE_COLS), f32)],
        scratch_shapes=[pltpu.VMEM((tm, D), bf16)],
        compiler_params=pltpu.CompilerParams(
            dimension_semantics=("arbitrary",), vmem_limit_bytes=VMEM_LIMIT),
        name="in_proj",
    )(x2, ada3, norm_w, w_big, w_gate, b_gate)


def _fox_cum_kernel(g_ref, f_ref):
    S = g_ref.shape[0]
    T = FOX_TILE
    row = lax.broadcasted_iota(jnp.int32, (T, T), 0)
    col = lax.broadcasted_iota(jnp.int32, (T, T), 1)
    lower = (col <= row).astype(bf16)
    carry = jnp.zeros((1, GATE_COLS), f32)
    for i in range(S // T):
        a1, a2, a3 = _split3(g_ref[i * T:(i + 1) * T, :])
        cs = (_dot(lower, a1) + _dot(lower, a2)) + _dot(lower, a3) + carry
        f_ref[i * T:(i + 1) * T, :] = jnp.concatenate(_split3(cs * (-LOG2E)), axis=1)
        carry = cs[T - 1:T, :]


def _fox_cum(g, batch, seq):
    return pl.pallas_call(
        _fox_cum_kernel,
        grid=(batch,),
        in_specs=[pl.BlockSpec((seq, GATE_COLS), lambda b: (b, 0))],
        out_specs=pl.BlockSpec((seq, 3 * GATE_COLS), lambda b: (b, 0)),
        out_shape=jax.ShapeDtypeStruct((batch * seq, 3 * GATE_COLS), bf16),
        name="fox_cum",
    )(g)


def _mlstm_kernel(q_ref, k_ref, v_ref, g_ref, nw_ref, out_ref, ct_ref, m_ref):
    L = MLSTM_CHUNK
    Dh = MLSTM_HEAD_DIM
    heads = range(MLSTM_HEADS)

    @pl.when(pl.program_id(1) == 0)
    def _():
        ct_ref[...] = jnp.zeros_like(ct_ref)
        m_ref[...] = jnp.zeros_like(m_ref)

    src_pos = lax.broadcasted_iota(jnp.int32, (L, L), 0)
    qry_pos = lax.broadcasted_iota(jnp.int32, (L, L), 1)
    visible = src_pos <= qry_pos
    lower = (qry_pos <= src_pos).astype(bf16)
    upper = visible.astype(bf16)

    g = g_ref[...]
    g1, g2, g3 = _split3(g)
    b_cols = (_dot(lower, g1) + _dot(lower, g2)) + _dot(lower, g3)
    gt = g.T[0:SUBLANES, :]
    t1, t2, t3 = _split3(gt)
    b_rows = (_dot(t1, upper) + _dot(t2, upper)) + _dot(t3, upper)

    def hs(h):
        return slice(h * Dh, (h + 1) * Dh)

    ones_rows = jnp.where(lax.broadcasted_iota(jnp.int32, (MLSTM_VROWS - Dh, L), 0) == 0, 1.0, 0.0)
    qs = [q_ref[:, hs(h)] * jnp.asarray(Dh ** -0.5, bf16) for h in heads]
    vts = [jnp.concatenate([v_ref[:, hs(h)].astype(f32).T, ones_rows], axis=0).astype(bf16)
           for h in heads]

    qk_t = [_dot_nt(k_ref[:, hs(h)], qs[h]) for h in heads]
    qc_t = [_dot_nt(ct_ref[h].astype(bf16), qs[h]) for h in heads]

    d_intra, d_inter, m_t = [], [], []
    for h in heads:
        i_row = gt[h:h + 1, :]
        b_row = b_rows[MLSTM_HEADS + h:MLSTM_HEADS + h + 1, :]
        a_col = g[:, h:h + 1] - b_cols[:, MLSTM_HEADS + h:MLSTM_HEADS + h + 1]
        m_prev = m_ref[h][0:1, 0:1]

        log_intra = jnp.where(visible, b_row + a_col, -jnp.inf)
        log_inter = b_row + m_prev
        m_h = jnp.maximum(log_inter, jnp.max(log_intra, axis=0, keepdims=True))
        m_t.append(m_h)
        d_intra.append(jnp.exp(log_intra - m_h))
        d_inter.append(jnp.exp(log_inter - m_h))

        b_last = b_row[:, L - 1:L]
        log_w = b_last - b_row + i_row
        m_new = jnp.maximum(b_last + m_prev, jnp.max(log_w, axis=1, keepdims=True))
        w_row = jnp.exp(log_w - m_new)
        decay = jnp.exp(b_last + m_prev - m_new)
        vw = vts[h] * w_row.astype(bf16)
        ct_ref[h] = decay * ct_ref[h] + _dot(vw, k_ref[:, hs(h)])
        m_ref[h] = jnp.broadcast_to(m_new, m_ref.shape[1:])

    for h in heads:
        p_t = (qk_t[h] * d_intra[h]).astype(bf16)
        tot = _dot(vts[h], p_t) + d_inter[h] * qc_t[h]
        den = tot[Dh:Dh + 1, :]
        hh = tot[0:Dh, :] * (1.0 / jnp.maximum(jnp.abs(den), jnp.exp(-m_t[h])))
        hh = hh * lax.rsqrt(jnp.mean(hh * hh, axis=0, keepdims=True) + EPS)
        out_ref[:, hs(h)] = (hh.T * nw_ref[:, hs(h)]).astype(bf16)


def _mlstm(p, g, norm_w, batch, seq):
    M, D = p.shape[0], D_MODEL
    L = MLSTM_CHUNK
    nc = seq // L

    def pspec(idx):
        return pl.BlockSpec((L, D), lambda b, c: (b * nc + c, idx))

    return pl.pallas_call(
        _mlstm_kernel,
        grid=(batch, nc),
        in_specs=[pspec(0), pspec(1), pspec(2),
                  pl.BlockSpec((L, GATE_COLS), lambda b, c: (b * nc + c, 0)),
                  pl.BlockSpec((1, D), lambda b, c: (0, 0))],
        out_specs=pl.BlockSpec((L, D), lambda b, c: (b * nc + c, 0)),
        out_shape=jax.ShapeDtypeStruct((M, D), bf16),
        scratch_shapes=[pltpu.VMEM((MLSTM_HEADS, MLSTM_VROWS, MLSTM_HEAD_DIM), f32),
                        pltpu.VMEM((MLSTM_HEADS, SUBLANES, LANES), f32)],
        compiler_params=pltpu.CompilerParams(
            dimension_semantics=("arbitrary", "arbitrary"), vmem_limit_bytes=VMEM_LIMIT),
        name="mlstm",
    )(p, p, p, g, norm_w)


def _fox_kernel(q_ref, k_ref, v_ref, f_ref, qw_ref, kw_ref, out_ref,
                qa_ref, ka_ref, vt_ref, acc_ref, m_ref):
    T = FOX_TILE
    Dh = FOX_HEAD_DIM
    S = q_ref.shape[0]
    pair = pl.program_id(1)

    r = lax.broadcasted_iota(jnp.int32, (LANES, LANES), 0) // Dh
    c = lax.broadcasted_iota(jnp.int32, (LANES, LANES), 1) // Dh
    head_mean = jnp.where(r == c, 1.0 / Dh, 0.0).astype(bf16)

    def head_rms(a, w):
        return a * lax.rsqrt(_dot((a * a).astype(bf16), head_mean) + EPS) * w

    qn = head_rms(q_ref[...].astype(f32), qw_ref[...] * (Dh ** -0.5 * LOG2E))
    kn = head_rms(k_ref[...].astype(f32), kw_ref[...])

    pieces = f_ref[...]
    rr = lax.broadcasted_iota(jnp.int32, (3 * LANES, LANES), 0)
    cc = lax.broadcasted_iota(jnp.int32, (3 * LANES, LANES), 1)
    piece = rr // LANES
    head = rr % LANES - 2 * MLSTM_HEADS - pair * FOX_PAIR
    place = ((head >= 0) & (head < FOX_PAIR) & (cc == (1 - head) * Dh + piece)).astype(bf16)
    bias = _dot(pieces, place)

    lane = lax.broadcasted_iota(jnp.int32, (S, LANES), 1)
    for hh in range(FOX_PAIR):
        own = lane // Dh == hh
        b0 = (1 - hh) * Dh
        ones = jnp.where((lane >= b0) & (lane < b0 + 3), 1.0, 0.0)
        qa_ref[hh] = jnp.where(own, qn, ones).astype(bf16)
        ka_ref[hh] = jnp.where(own, kn, bias).astype(bf16)
    v_t = v_ref[...].astype(f32).T
    ones_rows = jnp.where(lax.broadcasted_iota(jnp.int32, (FOX_VROWS - Dh, S), 0) == 0, 1.0, 0.0)
    for hh in range(FOX_PAIR):
        vt_ref[hh] = jnp.concatenate([v_t[hh * Dh:(hh + 1) * Dh, :], ones_rows], axis=0).astype(bf16)

    nt = S // T
    items = [(j, i, hh) for j in range(nt) for i in range(j, nt) for hh in range(FOX_PAIR)]

    def logits(item):
        j, i, hh = item
        s = _dot_nt(ka_ref[hh, j * T:(j + 1) * T, :], qa_ref[hh, i * T:(i + 1) * T, :])
        if i == j:
            kk = lax.broadcasted_iota(jnp.int32, (T, T), 0)
            qq = lax.broadcasted_iota(jnp.int32, (T, T), 1)
            s = jnp.where(kk <= qq, s, -jnp.inf)
        return s

    def accumulate(item, s):
        j, i, hh = item
        cols = slice(i * T, (i + 1) * T)
        st = slice(hh * nt + i, hh * nt + i + 1)
        m_cur = jnp.max(s, axis=0, keepdims=True)
        if j == 0:
            m_new = m_cur
        else:
            m_old = m_ref[st, :]
            m_new = jnp.maximum(m_old, m_cur)
            alpha = jnp.exp2(m_old - m_new)
        p = jnp.exp2((s - m_new).astype(bf16))
        pv = _dot(vt_ref[hh, :, j * T:(j + 1) * T], p)[0:FOX_AROWS, :]
        if j == 0:
            acc_ref[hh, :, cols] = pv
        else:
            acc_ref[hh, :, cols] = alpha * acc_ref[hh, :, cols] + pv
        m_ref[st, :] = m_new

    pending = []
    for n in range(len(items) + FOX_LOOKAHEAD):
        if n < len(items):
            pending.append((items[n], logits(items[n])))
        if n >= FOX_LOOKAHEAD:
            accumulate(*pending.pop(0))

    for i in range(nt):
        cols = slice(i * T, (i + 1) * T)
        o_t = jnp.concatenate(
            [acc_ref[hh, 0:Dh, cols] / acc_ref[hh, Dh:Dh + 1, cols] for hh in range(FOX_PAIR)], axis=0)
        out_ref[cols, :] = o_t.T.astype(bf16)


def _fox(p, fcum, q_norm_w, k_norm_w, batch, seq):
    M, D = p.shape[0], D_MODEL
    T = FOX_TILE
    nt = seq // T
    npairs = FOX_HEADS // FOX_PAIR

    def pspec(idx):
        return pl.BlockSpec((seq, LANES), lambda b, pr: (b, idx * npairs + pr))

    return pl.pallas_call(
        _fox_kernel,
        grid=(batch, npairs),
        in_specs=[pspec(4), pspec(5), pspec(6),
                  pl.BlockSpec((seq, 3 * GATE_COLS), lambda b, pr: (b, 0)),
                  pl.BlockSpec((1, LANES), lambda b, pr: (0, 0)),
                  pl.BlockSpec((1, LANES), lambda b, pr: (0, 0))],
        out_specs=pl.BlockSpec((seq, LANES), lambda b, pr: (b, pr)),
        out_shape=jax.ShapeDtypeStruct((M, D), bf16),
        scratch_shapes=[pltpu.VMEM((FOX_PAIR, seq, LANES), bf16),
                        pltpu.VMEM((FOX_PAIR, seq, LANES), bf16),
                        pltpu.VMEM((FOX_PAIR, FOX_VROWS, seq), bf16),
                        pltpu.VMEM((FOX_PAIR, FOX_AROWS, seq), f32),
                        pltpu.VMEM((FOX_PAIR * nt, T), f32)],
        compiler_params=pltpu.CompilerParams(
            dimension_semantics=("arbitrary", "arbitrary"), vmem_limit_bytes=VMEM_LIMIT),
        name="fox",
    )(p, p, p, fcum, q_norm_w, k_norm_w)


def _merge_kernel(x_ref, ada_ref, hm_ref, mo_ref, hf_ref, ga_ref, gb_ref, wa_ref, wb_ref, wo_ref, out_ref):
    hm = hm_ref[...].astype(f32) * jax.nn.sigmoid(mo_ref[...].astype(f32))
    ya = _dot(hm.astype(bf16), wa_ref[...])
    yb = _dot(hf_ref[...], wb_ref[...])
    merged = (jax.nn.sigmoid(ga_ref[...].astype(f32)) * ya
              + jax.nn.sigmoid(gb_ref[...].astype(f32)) * yb)
    y = _dot(merged.astype(bf16), wo_ref[...])
    out_ref[...] = x_ref[...] + ada_ref[2:3, :] * y


def _merge(x2, ada3, hm, hf, p, w_a, w_b, w_o, seq, tm):
    M, D = x2.shape
    tiles_per_seq = seq // tm
    row = pl.BlockSpec((tm, D), lambda m: (m, 0))
    wspec = pl.BlockSpec((D, D), lambda m: (0, 0), pipeline_mode=pl.Buffered(1))
    return pl.pallas_call(
        _merge_kernel,
        grid=(M // tm,),
        in_specs=[row,
                  pl.BlockSpec((None, 6, D), lambda m: (m // tiles_per_seq, 0, 0)),
                  row,
                  pl.BlockSpec((tm, D), lambda m: (m, 3)),
                  row,
                  pl.BlockSpec((tm, D), lambda m: (m, 7)),
                  pl.BlockSpec((tm, D), lambda m: (m, 8)),
                  wspec, wspec, wspec],
        out_specs=row,
        out_shape=jax.ShapeDtypeStruct((M, D), f32),
        compiler_params=pltpu.CompilerParams(
            dimension_semantics=("arbitrary",), vmem_limit_bytes=VMEM_LIMIT),
        name="merge",
    )(x2, ada3, hm, p, hf, p, p, w_a, w_b, w_o)


def _ffn_kernel(x_ref, ada_ref, nw_ref, wup_ref, cw_ref, cb_ref, wdown_ref, out_ref,
                h_ref, ubuf_ref, tail_ref, acc_ref, *, tiles_per_seq, tf):
    tm = x_ref.shape[0]
    H = SUBLANES
    nf = FFN_DIM // tf

    @pl.when(pl.program_id(0) == 0)
    def _():
        tail_ref[...] = jnp.zeros_like(tail_ref)

    first = (pl.program_id(0) % tiles_per_seq) == 0
    h_ref[...] = _rms_modulate(x_ref[...], nw_ref[...], ada_ref[3:4, :], ada_ref[4:5, :]).astype(bf16)

    def up(f):
        for part in range(2):
            cols = slice(part * FFN_DIM + f * tf, part * FFN_DIM + (f + 1) * tf)
            buf = ubuf_ref.at[f % FFN_SLOTS, part]
            u = _dot(h_ref[...], wup_ref[:, cols])
            buf[H:H + tm, :] = u
            buf[0:H, :] = jnp.where(first, 0.0, tail_ref[:, cols])
            tail_ref[:, cols] = u[tm - H:tm, :]

    def conv(f, part):
        cols = slice(part * FFN_DIM + f * tf, part * FFN_DIM + (f + 1) * tf)
        buf = ubuf_ref.at[f % FFN_SLOTS, part]
        return (buf[H:H + tm, :] * cw_ref[2:3, cols] + buf[H - 1:H - 1 + tm, :] * cw_ref[1:2, cols]
                + buf[H - 2:H - 2 + tm, :] * cw_ref[0:1, cols] + cb_ref[:, cols])

    def down(f):
        ug = conv(f, 0)
        a = ((ug * jax.nn.sigmoid(ug)) * conv(f, 1)).astype(bf16)
        y = _dot(a, wdown_ref[f * tf:(f + 1) * tf, :])
        if f == 0:
            acc_ref[...] = y
        else:
            acc_ref[...] += y

    for n in range(nf + FFN_LOOKAHEAD):
        if n < nf:
            up(n)
        if n >= FFN_LOOKAHEAD:
            down(n - FFN_LOOKAHEAD)
    out_ref[...] = x_ref[...] + ada_ref[5:6, :] * acc_ref[...]


def _ffn(x2, ada3, norm_w, w_up, conv_w, conv_b, w_down, seq, tm, tf):
    M, D = x2.shape
    F2 = w_up.shape[1]
    tiles_per_seq = seq // tm
    row = pl.BlockSpec((tm, D), lambda m: (m, 0))

    def const(shape):
        return pl.BlockSpec(shape, lambda m: (0, 0), pipeline_mode=pl.Buffered(1))

    return pl.pallas_call(
        functools.partial(_ffn_kernel, tiles_per_seq=tiles_per_seq, tf=tf),
        grid=(M // tm,),
        in_specs=[row,
                  pl.BlockSpec((None, 6, D), lambda m: (m // tiles_per_seq, 0, 0)),
                  const((1, D)), const((D, F2)), const((CONV_WIDTH, F2)), const((1, F2)),
                  const((F2 // 2, D))],
        out_specs=row,
        out_shape=jax.ShapeDtypeStruct((M, D), f32),
        scratch_shapes=[pltpu.VMEM((tm, D), bf16),
                        pltpu.VMEM((FFN_SLOTS, 2, tm + SUBLANES, tf), f32),
                        pltpu.VMEM((SUBLANES, F2), f32),
                        pltpu.VMEM((tm, D), f32)],
        compiler_params=pltpu.CompilerParams(
            dimension_semantics=("arbitrary",), vmem_limit_bytes=VMEM_LIMIT),
        name="ffn",
    )(x2, ada3, norm_w, w_up, conv_w, conv_b, w_down)


def _layer(x2, c, batch, seq, w_ada, b_ada, norm1_w, w_in, b_mlstm_i, b_mlstm_f, mlstm_norm_w,
           b_fox_f, fox_q_norm_w, fox_k_norm_w, w_branch_mlstm, w_branch_fox, w_out,
           norm2_w, w_up, conv_w, conv_b, w_down):
    D = D_MODEL
    ada3 = _ada(c, w_ada, b_ada.reshape(1, -1)).reshape(batch, 6, D)

    o_mi = 4 * D
    o_mf = o_mi + MLSTM_HEADS
    o_fq = o_mf + MLSTM_HEADS
    o_ff = o_fq + 3 * D
    o_ga = o_ff + FOX_HEADS
    w_in = w_in.astype(bf16)
    w_big = jnp.concatenate([w_in[:, :o_mi], w_in[:, o_fq:o_ff], w_in[:, o_ga:]], axis=1)
    n_gate = 2 * MLSTM_HEADS + FOX_HEADS
    w_gate = jnp.concatenate([w_in[:, o_mi:o_fq], w_in[:, o_ff:o_ga],
                              jnp.zeros((D, GATE_COLS - n_gate), bf16)], axis=1)
    b_gate = jnp.concatenate([b_mlstm_i, b_mlstm_f, b_fox_f,
                              jnp.zeros((GATE_COLS - n_gate,), f32)]).reshape(1, GATE_COLS)

    p, g = _in_proj(x2, ada3, norm1_w.reshape(1, D), w_big, w_gate, b_gate, seq, tm=512)
    fcum = _fox_cum(g, batch, seq)
    hm = _mlstm(p, g, mlstm_norm_w.reshape(1, D), batch, seq)
    hf = _fox(p, fcum, jnp.tile(fox_q_norm_w, FOX_PAIR).reshape(1, LANES),
              jnp.tile(fox_k_norm_w, FOX_PAIR).reshape(1, LANES), batch, seq)
    x2 = _merge(x2, ada3, hm, hf, p, w_branch_mlstm.astype(bf16), w_branch_fox.astype(bf16),
                w_out.astype(bf16), seq, tm=512)
    return _ffn(x2, ada3, norm2_w.reshape(1, D), w_up.astype(bf16), conv_w, conv_b.reshape(1, -1),
                w_down.astype(bf16), seq, tm=512, tf=256)


def kernel(x, c, w_ada, b_ada, norm1_w, w_in, b_mlstm_i, b_mlstm_f, mlstm_norm_w, b_fox_f,
           fox_q_norm_w, fox_k_norm_w, w_branch_mlstm, w_branch_fox, w_out, norm2_w, w_up,
           conv_w, conv_b, w_down):
    batch, seq, D = x.shape
    x2 = x.reshape(batch * seq, D)
    for l in range(w_ada.shape[0]):
        x2 = _layer(x2, c, batch, seq, w_ada[l], b_ada[l], norm1_w[l], w_in[l], b_mlstm_i[l],
                    b_mlstm_f[l], mlstm_norm_w[l], b_fox_f[l], fox_q_norm_w[l], fox_k_norm_w[l],
                    w_branch_mlstm[l], w_branch_fox[l], w_out[l], norm2_w[l], w_up[l],
                    conv_w[l], conv_b[l], w_down[l])
    return x2.reshape(batch, seq, D)
```

```python
import functools

import jax
import jax.numpy as jnp
from jax import lax
from jax.experimental import pallas as pl
from jax.experimental.pallas import tpu as pltpu

D_MODEL = 1024
MLSTM_HEADS = 4
MLSTM_HEAD_DIM = 256
FOX_HEADS = 16
FOX_HEAD_DIM = 64
FFN_DIM = 2816
CONV_WIDTH = 3
EPS = 1e-6

LANES = 128
SUBLANES = 8
N_BIG = 9
GATE_COLS = LANES
MLSTM_CHUNK = 256
MLSTM_ROWS = 4
MLSTM_VROWS = MLSTM_HEAD_DIM + 16
FOX_TILE = 256
FOX_PAIR = LANES // FOX_HEAD_DIM
FOX_PAIRS_PER_STEP = 2
FOX_VROWS = FOX_HEAD_DIM + 16
FOX_AROWS = FOX_HEAD_DIM + SUBLANES
FOX_LOOKAHEAD = 5
FFN_LOOKAHEAD = 3
FFN_SLOTS = FFN_LOOKAHEAD + 1
VMEM_LIMIT = 56 * 1024 * 1024
LOG2E = 1.4426950408889634

f32 = jnp.float32
bf16 = jnp.bfloat16


def _dot(a, b):
    return jnp.dot(a, b, preferred_element_type=f32)


def _dot_nt(a, b):
    return lax.dot_general(a, b, (((1,), (1,)), ((), ())), preferred_element_type=f32)


def _dot_tn(a, b):
    return lax.dot_general(a, b, (((0,), (0,)), ((), ())), preferred_element_type=f32)


def _split3(a):
    a1 = a.astype(bf16)
    r1 = a - a1.astype(f32)
    a2 = r1.astype(bf16)
    a3 = (r1 - a2.astype(f32)).astype(bf16)
    return a1, a2, a3


def _log_sigmoid(x):
    return jnp.minimum(x, 0.0) - jnp.log1p(jnp.exp(-jnp.abs(x)))


def _rms_modulate(x, norm_w, shift, scale):
    y = x * lax.rsqrt(jnp.mean(x * x, axis=-1, keepdims=True) + EPS)
    return (y * norm_w) * (1.0 + scale) + shift


def _ada_kernel(c_ref, w_ref, b_ref, out_ref):
    c = c_ref[...]
    a = (c * jax.nn.sigmoid(c)).astype(bf16)
    out_ref[...] = _dot(a, w_ref[...].astype(bf16)) + b_ref[...]


def _ada(c, w_ada, b_ada):
    B, D = c.shape
    n = w_ada.shape[1] // D
    return pl.pallas_call(
        _ada_kernel,
        grid=(n,),
        in_specs=[pl.BlockSpec((B, D), lambda j: (0, 0)),
                  pl.BlockSpec((D, D), lambda j: (0, j)),
                  pl.BlockSpec((1, D), lambda j: (0, j))],
        out_specs=pl.BlockSpec((B, D), lambda j: (0, j)),
        out_shape=jax.ShapeDtypeStruct((B, n * D), f32),
        name="ada",
    )(c, w_ada, b_ada)


def _in_proj_kernel(x_ref, ada_ref, nw_ref, wm_ref, wf_ref, wb_ref, wg_ref, bg_ref, p_ref, g_ref, h_ref):
    D = x_ref.shape[1]
    h = _rms_modulate(x_ref[...], nw_ref[...], ada_ref[0:1, :], ada_ref[1:2, :]).astype(bf16)
    h_ref[...] = h
    j = 0
    for w_ref in (wm_ref, wf_ref, wb_ref):
        for jj in range(w_ref.shape[1] // D):
            p_ref[:, j * D:(j + 1) * D] = _dot(h_ref[...], w_ref[:, jj * D:(jj + 1) * D]).astype(bf16)
            j += 1
    g = _dot(h_ref[...], wg_ref[...]) + bg_ref[...]
    col = lax.broadcasted_iota(jnp.int32, g.shape, 1)
    g_ref[...] = jnp.where(col < MLSTM_HEADS, g, _log_sigmoid(g))


def _in_proj(x2, ada3, norm_w, w_groups, w_gate, b_gate, seq, tm):
    M, D = x2.shape
    tiles_per_seq = seq // tm

    def const(shape):
        return pl.BlockSpec(shape, lambda m: (0, 0), pipeline_mode=pl.Buffered(1))

    return pl.pallas_call(
        _in_proj_kernel,
        grid=(M // tm,),
        in_specs=[pl.BlockSpec((tm, D), lambda m: (m, 0)),
                  pl.BlockSpec((None, 6, D), lambda m: (m // tiles_per_seq, 0, 0)),
                  const((1, D))] + [const(w.shape) for w in w_groups]
        + [const((D, GATE_COLS)), const((1, GATE_COLS))],
        out_specs=[pl.BlockSpec((tm, N_BIG * D), lambda m: (m, 0)),
                   pl.BlockSpec((tm, GATE_COLS), lambda m: (m, 0))],
        out_shape=[jax.ShapeDtypeStruct((M, N_BIG * D), bf16),
                   jax.ShapeDtypeStruct((M, GATE_COLS), f32)],
        scratch_shapes=[pltpu.VMEM((tm, D), bf16)],
        compiler_params=pltpu.CompilerParams(
            dimension_semantics=("arbitrary",), vmem_limit_bytes=VMEM_LIMIT),
        name="in_proj",
    )(x2, ada3, norm_w, *w_groups, w_gate, b_gate)


def _fox_cum_kernel(g_ref, f_ref):
    S = g_ref.shape[0]
    T = FOX_TILE
    row = lax.broadcasted_iota(jnp.int32, (T, T), 0)
    col = lax.broadcasted_iota(jnp.int32, (T, T), 1)
    lower = (col <= row).astype(bf16)
    carry = jnp.zeros((1, GATE_COLS), f32)
    for i in range(S // T):
        a1, a2, a3 = _split3(g_ref[i * T:(i + 1) * T, :])
        cs = (_dot(lower, a1) + _dot(lower, a2)) + _dot(lower, a3) + carry
        f_ref[i * T:(i + 1) * T, :] = jnp.concatenate(_split3(cs * (-LOG2E)), axis=1)
        carry = cs[T - 1:T, :]


def _fox_cum(g, batch, seq):
    return pl.pallas_call(
        _fox_cum_kernel,
        grid=(batch,),
        in_specs=[pl.BlockSpec((seq, GATE_COLS), lambda b: (b, 0))],
        out_specs=pl.BlockSpec((seq, 3 * GATE_COLS), lambda b: (b, 0)),
        out_shape=jax.ShapeDtypeStruct((batch * seq, 3 * GATE_COLS), bf16),
        name="fox_cum",
    )(g)


def _mlstm_kernel(q_ref, k_ref, v_ref, g_ref, nw_ref, out_ref, ct_ref, m_ref):
    L = MLSTM_CHUNK
    Dh = MLSTM_HEAD_DIM
    streams = [(r, h) for r in range(MLSTM_ROWS) for h in range(MLSTM_HEADS)]

    @pl.when(pl.program_id(1) == 0)
    def _():
        ct_ref[...] = jnp.zeros_like(ct_ref)
        m_ref[...] = jnp.zeros_like(m_ref)

    src_pos = lax.broadcasted_iota(jnp.int32, (L, L), 0)
    qry_pos = lax.broadcasted_iota(jnp.int32, (L, L), 1)
    visible = src_pos <= qry_pos
    lower = (qry_pos <= src_pos).astype(bf16)
    upper = visible.astype(bf16)
    ones_rows = jnp.where(lax.broadcasted_iota(jnp.int32, (MLSTM_VROWS - Dh, L), 0) == 0, 1.0, 0.0)

    def hs(h):
        return slice(h * Dh, (h + 1) * Dh)

    gs, gts, b_cols, b_rows = [], [], [], []
    for r in range(MLSTM_ROWS):
        g = g_ref[r]
        g1, g2, g3 = _split3(g)
        b_cols.append((_dot(lower, g1) + _dot(lower, g2)) + _dot(lower, g3))
        gt = g.T[0:SUBLANES, :]
        t1, t2, t3 = _split3(gt)
        b_rows.append((_dot(t1, upper) + _dot(t2, upper)) + _dot(t3, upper))
        gs.append(g)
        gts.append(gt)

    qs = [q_ref[r, :, hs(h)] * jnp.asarray(Dh ** -0.5, bf16) for r, h in streams]
    vts = [jnp.concatenate([v_ref[r, :, hs(h)].astype(f32).T, ones_rows], axis=0).astype(bf16)
           for r, h in streams]

    qk_t = [_dot_nt(k_ref[r, :, hs(h)], qs[st]) for st, (r, h) in enumerate(streams)]
    qc_t = [_dot_nt(ct_ref[st].astype(bf16), qs[st]) for st in range(len(streams))]

    d_intra, d_inter, m_t = [], [], []
    for st, (r, h) in enumerate(streams):
        i_row = gts[r][h:h + 1, :]
        b_row = b_rows[r][MLSTM_HEADS + h:MLSTM_HEADS + h + 1, :]
        a_col = gs[r][:, h:h + 1] - b_cols[r][:, MLSTM_HEADS + h:MLSTM_HEADS + h + 1]
        m_prev = m_ref[st][0:1, 0:1]

        log_intra = jnp.where(visible, b_row + a_col, -jnp.inf)
        log_inter = b_row + m_prev
        m_h = jnp.maximum(log_inter, jnp.max(log_intra, axis=0, keepdims=True))
        m_t.append(m_h)
        d_intra.append(jnp.exp(log_intra - m_h))
        d_inter.append(jnp.exp(log_inter - m_h))

        b_last = b_row[:, L - 1:L]
        log_w = b_last - b_row + i_row
        m_new = jnp.maximum(b_last + m_prev, jnp.max(log_w, axis=1, keepdims=True))
        w_row = jnp.exp(log_w - m_new)
        decay = jnp.exp(b_last + m_prev - m_new)
        vw = vts[st] * w_row.astype(bf16)
        ct_ref[st] = decay * ct_ref[st] + _dot(vw, k_ref[r, :, hs(h)])
        m_ref[st] = jnp.broadcast_to(m_new, m_ref.shape[1:])

    for st, (r, h) in enumerate(streams):
        p_t = (qk_t[st] * d_intra[st]).astype(bf16)
        tot = _dot(vts[st], p_t) + d_inter[st] * qc_t[st]
        den = tot[Dh:Dh + 1, :]
        hh = tot[0:Dh, :] * (1.0 / jnp.maximum(jnp.abs(den), jnp.exp(-m_t[st])))
        hh = hh * lax.rsqrt(jnp.mean(hh * hh, axis=0, keepdims=True) + EPS)
        out_ref[r, :, hs(h)] = (hh.T * nw_ref[:, hs(h)]).astype(bf16)


def _mlstm(p, g, norm_w, batch, seq):
    M, D = p.shape[0], D_MODEL
    L = MLSTM_CHUNK
    R = MLSTM_ROWS
    nstreams = R * MLSTM_HEADS
    p3 = p.reshape(batch, seq, p.shape[1])

    def pspec(idx):
        return pl.BlockSpec((R, L, D), lambda b, c: (b, c, idx))

    out = pl.pallas_call(
        _mlstm_kernel,
        grid=(batch // R, seq // L),
        in_specs=[pspec(0), pspec(1), pspec(2),
                  pl.BlockSpec((R, L, GATE_COLS), lambda b, c: (b, c, 0)),
                  pl.BlockSpec((1, D), lambda b, c: (0, 0))],
        out_specs=pl.BlockSpec((R, L, D), lambda b, c: (b, c, 0)),
        out_shape=jax.ShapeDtypeStruct((batch, seq, D), bf16),
        scratch_shapes=[pltpu.VMEM((nstreams, MLSTM_VROWS, MLSTM_HEAD_DIM), f32),
                        pltpu.VMEM((nstreams, SUBLANES, LANES), f32)],
        compiler_params=pltpu.CompilerParams(
            dimension_semantics=("arbitrary", "arbitrary"), vmem_limit_bytes=VMEM_LIMIT),
        name="mlstm",
    )(p3, p3, p3, g.reshape(batch, seq, GATE_COLS), norm_w)
    return out.reshape(M, D)


def _fox_kernel(q_ref, k_ref, v_ref, f_ref, qw_ref, kw_ref, out_ref,
                qa_ref, ka_ref, vt_ref, acc_ref, m_ref):
    T = FOX_TILE
    Dh = FOX_HEAD_DIM
    S = q_ref.shape[0]
    nt = S // T

    r = lax.broadcasted_iota(jnp.int32, (LANES, LANES), 0) // Dh
    c = lax.broadcasted_iota(jnp.int32, (LANES, LANES), 1) // Dh
    head_mean = jnp.where(r == c, 1.0 / Dh, 0.0).astype(bf16)
    rr = lax.broadcasted_iota(jnp.int32, (3 * LANES, LANES), 0)
    cc = lax.broadcasted_iota(jnp.int32, (3 * LANES, LANES), 1)
    lane = lax.broadcasted_iota(jnp.int32, (S, LANES), 1)
    ones_rows = jnp.where(lax.broadcasted_iota(jnp.int32, (FOX_VROWS - Dh, S), 0) == 0, 1.0, 0.0)

    def head_rms(a, w):
        return a * lax.rsqrt(_dot((a * a).astype(bf16), head_mean) + EPS) * w

    def prepare(sub):
        ls = slice(sub * LANES, (sub + 1) * LANES)
        pair = pl.program_id(1) * FOX_PAIRS_PER_STEP + sub
        qn = head_rms(q_ref[:, ls].astype(f32), qw_ref[...] * (Dh ** -0.5 * LOG2E))
        kn = head_rms(k_ref[:, ls].astype(f32), kw_ref[...])
        head = rr % LANES - 2 * MLSTM_HEADS - pair * FOX_PAIR
        place = ((head >= 0) & (head < FOX_PAIR) & (cc == (1 - head) * Dh + rr // LANES)).astype(bf16)
        bias = _dot(f_ref[...], place)
        v_t = v_ref[:, ls].astype(f32).T
        for hh in range(FOX_PAIR):
            hd = sub * FOX_PAIR + hh
            own = lane // Dh == hh
            b0 = (1 - hh) * Dh
            ones = jnp.where((lane >= b0) & (lane < b0 + 3), 1.0, 0.0)
            qa_ref[hd] = jnp.where(own, qn, ones).astype(bf16)
            ka_ref[hd] = jnp.where(own, kn, bias).astype(bf16)
            vt_ref[hd] = jnp.concatenate([v_t[hh * Dh:(hh + 1) * Dh, :], ones_rows], axis=0).astype(bf16)

    def logits(item):
        j, i, hd = item
        s = _dot_nt(ka_ref[hd, j * T:(j + 1) * T, :], qa_ref[hd, i * T:(i + 1) * T, :])
        if i == j:
            kk = lax.broadcasted_iota(jnp.int32, (T, T), 0)
            qq = lax.broadcasted_iota(jnp.int32, (T, T), 1)
            s = jnp.where(kk <= qq, s, -jnp.inf)
        return s

    def accumulate(item, s):
        j, i, hd = item
        cols = slice(i * T, (i + 1) * T)
        st = slice(hd * nt + i, hd * nt + i + 1)
        m_cur = jnp.max(s, axis=0, keepdims=True)
        if j == 0:
            m_new = m_cur
        else:
            m_old = m_ref[st, :]
            m_new = jnp.maximum(m_old, m_cur)
            alpha = jnp.exp2(m_old - m_new)
        p = jnp.exp2((s - m_new).astype(bf16))
        pv = _dot(vt_ref[hd, :, j * T:(j + 1) * T], p)[0:FOX_AROWS, :]
        if j == 0:
            acc_ref[hd, :, cols] = pv
        else:
            acc_ref[hd, :, cols] = alpha * acc_ref[hd, :, cols] + pv
        m_ref[st, :] = m_new

    for sub in range(FOX_PAIRS_PER_STEP):
        prepare(sub)

    items = [(j, i, sub * FOX_PAIR + hh) for sub in range(FOX_PAIRS_PER_STEP)
             for j in range(nt) for i in range(j, nt) for hh in range(FOX_PAIR)]
    pending = []
    for n in range(len(items) + FOX_LOOKAHEAD):
        if n < len(items):
            pending.append((items[n], logits(items[n])))
        if n >= FOX_LOOKAHEAD:
            accumulate(*pending.pop(0))

    for sub in range(FOX_PAIRS_PER_STEP):
        for i in range(nt):
            cols = slice(i * T, (i + 1) * T)
            o_t = jnp.concatenate(
                [acc_ref[hd, 0:Dh, cols] / acc_ref[hd, Dh:Dh + 1, cols]
                 for hd in range(sub * FOX_PAIR, (sub + 1) * FOX_PAIR)], axis=0)
            out_ref[cols, sub * LANES:(sub + 1) * LANES] = o_t.T.astype(bf16)


def _fox(p, fcum, q_norm_w, k_norm_w, batch, seq):
    M, D = p.shape[0], D_MODEL
    T = FOX_TILE
    nt = seq // T
    nsteps = FOX_HEADS // (FOX_PAIR * FOX_PAIRS_PER_STEP)
    heads = FOX_PAIR * FOX_PAIRS_PER_STEP
    W = LANES * FOX_PAIRS_PER_STEP

    def pspec(idx):
        return pl.BlockSpec((seq, W), lambda b, pr: (b, idx * nsteps + pr))

    return pl.pallas_call(
        _fox_kernel,
        grid=(batch, nsteps),
        in_specs=[pspec(4), pspec(5), pspec(6),
                  pl.BlockSpec((seq, 3 * GATE_COLS), lambda b, pr: (b, 0)),
                  pl.BlockSpec((1, LANES), lambda b, pr: (0, 0)),
                  pl.BlockSpec((1, LANES), lambda b, pr: (0, 0))],
        out_specs=pl.BlockSpec((seq, W), lambda b, pr: (b, pr)),
        out_shape=jax.ShapeDtypeStruct((M, D), bf16),
        scratch_shapes=[pltpu.VMEM((heads, seq, LANES), bf16),
                        pltpu.VMEM((heads, seq, LANES), bf16),
                        pltpu.VMEM((heads, FOX_VROWS, seq), bf16),
                        pltpu.VMEM((heads, FOX_AROWS, seq), f32),
                        pltpu.VMEM((heads * nt, T), f32)],
        compiler_params=pltpu.CompilerParams(
            dimension_semantics=("arbitrary", "arbitrary"), vmem_limit_bytes=VMEM_LIMIT),
        name="fox",
    )(p, p, p, fcum, q_norm_w, k_norm_w)


def _merge_kernel(x_ref, ada_ref, hm_ref, mo_ref, hf_ref, ga_ref, gb_ref, wa_ref, wb_ref, wo_ref, out_ref):
    hm = hm_ref[...].astype(f32) * jax.nn.sigmoid(mo_ref[...].astype(f32))
    ya = _dot(hm.astype(bf16), wa_ref[...])
    yb = _dot(hf_ref[...], wb_ref[...])
    merged = (jax.nn.sigmoid(ga_ref[...].astype(f32)) * ya
              + jax.nn.sigmoid(gb_ref[...].astype(f32)) * yb)
    y = _dot(merged.astype(bf16), wo_ref[...])
    out_ref[...] = x_ref[...] + ada_ref[2:3, :] * y


def _merge(x2, ada3, hm, hf, p, w_a, w_b, w_o, seq, tm):
    M, D = x2.shape
    tiles_per_seq = seq // tm
    row = pl.BlockSpec((tm, D), lambda m: (m, 0))
    wspec = pl.BlockSpec((D, D), lambda m: (0, 0), pipeline_mode=pl.Buffered(1))
    return pl.pallas_call(
        _merge_kernel,
        grid=(M // tm,),
        in_specs=[row,
                  pl.BlockSpec((None, 6, D), lambda m: (m // tiles_per_seq, 0, 0)),
                  row,
                  pl.BlockSpec((tm, D), lambda m: (m, 3)),
                  row,
                  pl.BlockSpec((tm, D), lambda m: (m, 7)),
                  pl.BlockSpec((tm, D), lambda m: (m, 8)),
                  wspec, wspec, wspec],
        out_specs=row,
        out_shape=jax.ShapeDtypeStruct((M, D), f32),
        compiler_params=pltpu.CompilerParams(
            dimension_semantics=("arbitrary",), vmem_limit_bytes=VMEM_LIMIT),
        name="merge",
    )(x2, ada3, hm, p, hf, p, p, w_a, w_b, w_o)


def _ffn_kernel(x_ref, ada_ref, nw_ref, wup_ref, cw_ref, cb_ref, wdown_ref, out_ref,
                h_ref, tail_ref, acc_ref, *ubuf_refs, tiles_per_seq, tf):
    tm = x_ref.shape[0]
    H = SUBLANES
    nf = FFN_DIM // tf

    @pl.when(pl.program_id(0) == 0)
    def _():
        tail_ref[...] = jnp.zeros_like(tail_ref)

    first = (pl.program_id(0) % tiles_per_seq) == 0
    h_ref[...] = _rms_modulate(x_ref[...], nw_ref[...], ada_ref[3:4, :], ada_ref[4:5, :]).astype(bf16)

    def up(f):
        for part in range(2):
            cols = slice(part * FFN_DIM + f * tf, part * FFN_DIM + (f + 1) * tf)
            buf = ubuf_refs[2 * (f % FFN_SLOTS) + part]
            u = _dot(h_ref[...], wup_ref[:, cols])
            buf[H:H + tm, :] = u
            buf[0:H, :] = jnp.where(first, 0.0, tail_ref[:, cols])
            tail_ref[:, cols] = u[tm - H:tm, :]

    def conv(f, part):
        cols = slice(part * FFN_DIM + f * tf, part * FFN_DIM + (f + 1) * tf)
        buf = ubuf_refs[2 * (f % FFN_SLOTS) + part]
        return (buf[H:H + tm, :] * cw_ref[2:3, cols] + buf[H - 1:H - 1 + tm, :] * cw_ref[1:2, cols]
                + buf[H - 2:H - 2 + tm, :] * cw_ref[0:1, cols] + cb_ref[:, cols])

    def down(f):
        ug = conv(f, 0)
        a = ((ug * jax.nn.sigmoid(ug)) * conv(f, 1)).astype(bf16)
        y = _dot(a, wdown_ref[f * tf:(f + 1) * tf, :])
        if f == 0:
            acc_ref[...] = y
        else:
            acc_ref[...] += y

    for n in range(nf + FFN_LOOKAHEAD):
        if n < nf:
            up(n)
        if n >= FFN_LOOKAHEAD:
            down(n - FFN_LOOKAHEAD)
    out_ref[...] = x_ref[...] + ada_ref[5:6, :] * acc_ref[...]


def _ffn(x2, ada3, norm_w, w_up, conv_w, conv_b, w_down, seq, tm, tf):
    M, D = x2.shape
    F2 = w_up.shape[1]
    tiles_per_seq = seq // tm
    row = pl.BlockSpec((tm, D), lambda m: (m, 0))

    def const(shape):
        return pl.BlockSpec(shape, lambda m: (0, 0), pipeline_mode=pl.Buffered(1))

    return pl.pallas_call(
        functools.partial(_ffn_kernel, tiles_per_seq=tiles_per_seq, tf=tf),
        grid=(M // tm,),
        in_specs=[row,
                  pl.BlockSpec((None, 6, D), lambda m: (m // tiles_per_seq, 0, 0)),
                  const((1, D)), const((D, F2)), const((CONV_WIDTH, F2)), const((1, F2)),
                  const((F2 // 2, D))],
        out_specs=row,
        out_shape=jax.ShapeDtypeStruct((M, D), f32),
        scratch_shapes=[pltpu.VMEM((tm, D), bf16),
                        pltpu.VMEM((SUBLANES, F2), f32),
                        pltpu.VMEM((tm, D), f32)]
        + [pltpu.VMEM((tm + SUBLANES, tf), f32) for _ in range(2 * FFN_SLOTS)],
        compiler_params=pltpu.CompilerParams(
            dimension_semantics=("arbitrary",), vmem_limit_bytes=VMEM_LIMIT),
        name="ffn",
    )(x2, ada3, norm_w, w_up, conv_w, conv_b, w_down)


def _layer(x2, c, batch, seq, w_ada, b_ada, norm1_w, w_in, b_mlstm_i, b_mlstm_f, mlstm_norm_w,
           b_fox_f, fox_q_norm_w, fox_k_norm_w, w_branch_mlstm, w_branch_fox, w_out,
           norm2_w, w_up, conv_w, conv_b, w_down):
    D = D_MODEL
    ada3 = _ada(c, w_ada, b_ada.reshape(1, -1)).reshape(batch, 6, D)

    o_mi = 4 * D
    o_mf = o_mi + MLSTM_HEADS
    o_fq = o_mf + MLSTM_HEADS
    o_ff = o_fq + 3 * D
    o_ga = o_ff + FOX_HEADS
    w_groups = [w_in[:, :o_mi].astype(bf16), w_in[:, o_fq:o_ff].astype(bf16), w_in[:, o_ga:].astype(bf16)]
    n_gate = 2 * MLSTM_HEADS + FOX_HEADS
    w_gate = jnp.concatenate([w_in[:, o_mi:o_fq], w_in[:, o_ff:o_ga],
                              jnp.zeros((D, GATE_COLS - n_gate), f32)], axis=1).astype(bf16)
    b_gate = jnp.concatenate([b_mlstm_i, b_mlstm_f, b_fox_f,
                              jnp.zeros((GATE_COLS - n_gate,), f32)]).reshape(1, GATE_COLS)

    p, g = _in_proj(x2, ada3, norm1_w.reshape(1, D), w_groups, w_gate, b_gate, seq, tm=512)
    fcum = _fox_cum(g, batch, seq)
    hm = _mlstm(p, g, mlstm_norm_w.reshape(1, D), batch, seq)
    hf = _fox(p, fcum, jnp.tile(fox_q_norm_w, FOX_PAIR).reshape(1, LANES),
              jnp.tile(fox_k_norm_w, FOX_PAIR).reshape(1, LANES), batch, seq)
    x2 = _merge(x2, ada3, hm, hf, p, w_branch_mlstm.astype(bf16), w_branch_fox.astype(bf16),
                w_out.astype(bf16), seq, tm=512)
    return _ffn(x2, ada3, norm2_w.reshape(1, D), w_up.astype(bf16), conv_w, conv_b.reshape(1, -1),
                w_down.astype(bf16), seq, tm=512, tf=256)


def kernel(x, c, w_ada, b_ada, norm1_w, w_in, b_mlstm_i, b_mlstm_f, mlstm_norm_w, b_fox_f,
           fox_q_norm_w, fox_k_norm_w, w_branch_mlstm, w_branch_fox, w_out, norm2_w, w_up,
           conv_w, conv_b, w_down):
    batch, seq, D = x.shape
    x2 = x.reshape(batch * seq, D)
    for l in range(w_ada.shape[0]):
        x2 = _layer(x2, c, batch, seq, w_ada[l], b_ada[l], norm1_w[l], w_in[l], b_mlstm_i[l],
                    b_mlstm_f[l], mlstm_norm_w[l], b_fox_f[l], fox_q_norm_w[l], fox_k_norm_w[l],
                    w_branch_mlstm[l], w_branch_fox[l], w_out[l], norm2_w[l], w_up[l],
                    conv_w[l], conv_b[l], w_down[l])
    return x2.reshape(batch, seq, D)
```

```python
import functools

import jax
import jax.numpy as jnp
from jax import lax
from jax.experimental import pallas as pl
from jax.experimental.pallas import tpu as pltpu

D_MODEL = 1024
MLSTM_HEADS = 4
MLSTM_HEAD_DIM = 256
FOX_HEADS = 16
FOX_HEAD_DIM = 64
FFN_DIM = 2816
CONV_WIDTH = 3
EPS = 1e-6

LANES = 128
SUBLANES = 8
GATE_COLS = LANES
MLSTM_CHUNK = 256
MLSTM_ROWS = 4
MLSTM_VROWS = MLSTM_HEAD_DIM + 16
FOX_TILE = 256
FOX_PAIR = LANES // FOX_HEAD_DIM
FOX_PAIRS_PER_STEP = 2
FOX_WIDTH = LANES * FOX_PAIRS_PER_STEP
FOX_VROWS = FOX_HEAD_DIM + 16
FOX_AROWS = FOX_HEAD_DIM + SUBLANES
FOX_LOOKAHEAD = 5
FFN_LOOKAHEAD = 3
FFN_SLOTS = FFN_LOOKAHEAD + 1
FFN_SUBTILES = 1
VMEM_LIMIT = 56 * 1024 * 1024
LOG2E = 1.4426950408889634

f32 = jnp.float32
bf16 = jnp.bfloat16


def _dot(a, b):
    return jnp.dot(a, b, preferred_element_type=f32)


def _dot_nt(a, b):
    return lax.dot_general(a, b, (((1,), (1,)), ((), ())), preferred_element_type=f32)


def _dot_tn(a, b):
    return lax.dot_general(a, b, (((0,), (0,)), ((), ())), preferred_element_type=f32)


def _split3(a):
    a1 = a.astype(bf16)
    r1 = a - a1.astype(f32)
    a2 = r1.astype(bf16)
    a3 = (r1 - a2.astype(f32)).astype(bf16)
    return a1, a2, a3


def _log_sigmoid(x):
    return jnp.minimum(x, 0.0) - jnp.log1p(jnp.exp(-jnp.abs(x)))


def _rms_modulate(x, norm_w, shift, scale):
    y = x * lax.rsqrt(jnp.mean(x * x, axis=-1, keepdims=True) + EPS)
    return (y * norm_w) * (1.0 + scale) + shift


def _ada_kernel(c_ref, w_ref, b_ref, out_ref):
    c = c_ref[...]
    a = (c * jax.nn.sigmoid(c)).astype(bf16)
    out_ref[...] = _dot(a, w_ref[...].astype(bf16)) + b_ref[...]


def _ada(c, w_ada, b_ada):
    B, D = c.shape
    n = w_ada.shape[1] // D
    return pl.pallas_call(
        _ada_kernel,
        grid=(n,),
        in_specs=[pl.BlockSpec((B, D), lambda j: (0, 0)),
                  pl.BlockSpec((D, D), lambda j: (0, j)),
                  pl.BlockSpec((1, D), lambda j: (0, j))],
        out_specs=pl.BlockSpec((B, D), lambda j: (0, j)),
        out_shape=jax.ShapeDtypeStruct((B, n * D), f32),
        name="ada",
    )(c, w_ada, b_ada)


def _in_proj_kernel(x_ref, ada_ref, nw_ref, wm_ref, wf_ref, wb_ref, wg_ref, bg_ref,
                    p_ref, pf_ref, g_ref, h_ref):
    D = x_ref.shape[1]
    W = pf_ref.shape[2]
    h = _rms_modulate(x_ref[...], nw_ref[...], ada_ref[0:1, :], ada_ref[1:2, :]).astype(bf16)
    h_ref[...] = h
    j = 0
    for w_ref in (wm_ref, wb_ref):
        for jj in range(w_ref.shape[0] // D):
            p_ref[:, j * D:(j + 1) * D] = _dot_nt(h_ref[...], w_ref[jj * D:(jj + 1) * D, :]).astype(bf16)
            j += 1
    for jj in range(wf_ref.shape[0] // D):
        y = _dot_nt(h_ref[...], wf_ref[jj * D:(jj + 1) * D, :]).astype(bf16)
        for s in range(D // W):
            pf_ref[jj * (D // W) + s] = y[:, s * W:(s + 1) * W]
    g = _dot_nt(h_ref[...], wg_ref[...]) + bg_ref[...]
    col = lax.broadcasted_iota(jnp.int32, g.shape, 1)
    g_ref[...] = jnp.where(col < MLSTM_HEADS, g, _log_sigmoid(g))


def _in_proj(x2, ada3, norm_w, w_groups, w_gate, b_gate, seq, tm):
    M, D = x2.shape
    tiles_per_seq = seq // tm
    n_main = (w_groups[0].shape[0] + w_groups[2].shape[0]) // D
    n_slab = w_groups[1].shape[0] // FOX_WIDTH

    def const(shape):
        return pl.BlockSpec(shape, lambda m: (0, 0), pipeline_mode=pl.Buffered(1))

    return pl.pallas_call(
        _in_proj_kernel,
        grid=(M // tm,),
        in_specs=[pl.BlockSpec((tm, D), lambda m: (m, 0)),
                  pl.BlockSpec((None, 6, D), lambda m: (m // tiles_per_seq, 0, 0)),
                  const((1, D))] + [const(w.shape) for w in w_groups]
        + [const((GATE_COLS, D)), const((1, GATE_COLS))],
        out_specs=[pl.BlockSpec((tm, n_main * D), lambda m: (m, 0)),
                   pl.BlockSpec((n_slab, tm, FOX_WIDTH), lambda m: (0, m, 0)),
                   pl.BlockSpec((tm, GATE_COLS), lambda m: (m, 0))],
        out_shape=[jax.ShapeDtypeStruct((M, n_main * D), bf16),
                   jax.ShapeDtypeStruct((n_slab, M, FOX_WIDTH), bf16),
                   jax.ShapeDtypeStruct((M, GATE_COLS), f32)],
        scratch_shapes=[pltpu.VMEM((tm, D), bf16)],
        compiler_params=pltpu.CompilerParams(
            dimension_semantics=("arbitrary",), vmem_limit_bytes=VMEM_LIMIT),
        name="in_proj",
    )(x2, ada3, norm_w, *w_groups, w_gate, b_gate)


def _fox_cum_kernel(g_ref, f_ref):
    S = g_ref.shape[0]
    T = FOX_TILE
    row = lax.broadcasted_iota(jnp.int32, (T, T), 0)
    col = lax.broadcasted_iota(jnp.int32, (T, T), 1)
    lower = (col <= row).astype(bf16)
    carry = jnp.zeros((1, GATE_COLS), f32)
    for i in range(S // T):
        a1, a2, a3 = _split3(g_ref[i * T:(i + 1) * T, :])
        cs = (_dot(lower, a1) + _dot(lower, a2)) + _dot(lower, a3) + carry
        f_ref[i * T:(i + 1) * T, :] = jnp.concatenate(_split3(cs * (-LOG2E)), axis=1)
        carry = cs[T - 1:T, :]


def _fox_cum(g, batch, seq):
    return pl.pallas_call(
        _fox_cum_kernel,
        grid=(batch,),
        in_specs=[pl.BlockSpec((seq, GATE_COLS), lambda b: (b, 0))],
        out_specs=pl.BlockSpec((seq, 3 * GATE_COLS), lambda b: (b, 0)),
        out_shape=jax.ShapeDtypeStruct((batch * seq, 3 * GATE_COLS), bf16),
        name="fox_cum",
    )(g)


def _mlstm_kernel(q_ref, k_ref, v_ref, g_ref, nw_ref, out_ref, ct_ref, m_ref):
    L = MLSTM_CHUNK
    Dh = MLSTM_HEAD_DIM
    streams = [(r, h) for r in range(MLSTM_ROWS) for h in range(MLSTM_HEADS)]

    @pl.when(pl.program_id(1) == 0)
    def _():
        ct_ref[...] = jnp.zeros_like(ct_ref)
        m_ref[...] = jnp.zeros_like(m_ref)

    src_pos = lax.broadcasted_iota(jnp.int32, (L, L), 0)
    qry_pos = lax.broadcasted_iota(jnp.int32, (L, L), 1)
    visible = src_pos <= qry_pos
    lower = (qry_pos <= src_pos).astype(bf16)
    upper = visible.astype(bf16)
    ones_rows = jnp.where(lax.broadcasted_iota(jnp.int32, (MLSTM_VROWS - Dh, L), 0) == 0, 1.0, 0.0)

    def hs(h):
        return slice(h * Dh, (h + 1) * Dh)

    gs, gts, b_cols, b_rows = [], [], [], []
    for r in range(MLSTM_ROWS):
        g = g_ref[r]
        g1, g2, g3 = _split3(g)
        b_cols.append((_dot(lower, g1) + _dot(lower, g2)) + _dot(lower, g3))
        gt = g.T[0:SUBLANES, :]
        t1, t2, t3 = _split3(gt)
        b_rows.append((_dot(t1, upper) + _dot(t2, upper)) + _dot(t3, upper))
        gs.append(g)
        gts.append(gt)

    qs = [q_ref[r, :, hs(h)] * jnp.asarray(Dh ** -0.5, bf16) for r, h in streams]
    vts = [jnp.concatenate([v_ref[r, :, hs(h)].astype(f32).T, ones_rows], axis=0).astype(bf16)
           for r, h in streams]

    qk_t = [_dot_nt(k_ref[r, :, hs(h)], qs[st]) for st, (r, h) in enumerate(streams)]
    qc_t = [_dot_nt(ct_ref[st].astype(bf16), qs[st]) for st in range(len(streams))]

    d_intra, d_inter, m_t = [], [], []
    for st, (r, h) in enumerate(streams):
        i_row = gts[r][h:h + 1, :]
        b_row = b_rows[r][MLSTM_HEADS + h:MLSTM_HEADS + h + 1, :]
        a_col = gs[r][:, h:h + 1] - b_cols[r][:, MLSTM_HEADS + h:MLSTM_HEADS + h + 1]
        m_prev = m_ref[st][0:1, 0:1]

        log_intra = jnp.where(visible, b_row + a_col, -jnp.inf)
        log_inter = b_row + m_prev
        m_h = jnp.maximum(log_inter, jnp.max(log_intra, axis=0, keepdims=True))
        m_t.append(m_h)
        d_intra.append(jnp.exp(log_intra - m_h))
        d_inter.append(jnp.exp(log_inter - m_h))

        b_last = b_row[:, L - 1:L]
        log_w = b_last - b_row + i_row
        m_new = jnp.maximum(b_last + m_prev, jnp.max(log_w, axis=1, keepdims=True))
        w_row = jnp.exp(log_w - m_new)
        decay = jnp.exp(b_last + m_prev - m_new)
        vw = vts[st] * w_row.astype(bf16)
        ct_ref[st] = decay * ct_ref[st] + _dot(vw, k_ref[r, :, hs(h)])
        m_ref[st] = jnp.broadcast_to(m_new, m_ref.shape[1:])

    for st, (r, h) in enumerate(streams):
        p_t = (qk_t[st] * d_intra[st]).astype(bf16)
        tot = _dot(vts[st], p_t) + d_inter[st] * qc_t[st]
        den = tot[Dh:Dh + 1, :]
        hh = tot[0:Dh, :] * (1.0 / jnp.maximum(jnp.abs(den), jnp.exp(-m_t[st])))
        hh = hh * lax.rsqrt(jnp.mean(hh * hh, axis=0, keepdims=True) + EPS)
        out_ref[r, :, hs(h)] = (hh.T * nw_ref[:, hs(h)]).astype(bf16)


def _mlstm(p, g, norm_w, batch, seq):
    M, D = p.shape[0], D_MODEL
    L = MLSTM_CHUNK
    R = MLSTM_ROWS
    nstreams = R * MLSTM_HEADS
    p3 = p.reshape(batch, seq, p.shape[1])

    def pspec(idx):
        return pl.BlockSpec((R, L, D), lambda b, c: (b, c, idx))

    out = pl.pallas_call(
        _mlstm_kernel,
        grid=(batch // R, seq // L),
        in_specs=[pspec(0), pspec(1), pspec(2),
                  pl.BlockSpec((R, L, GATE_COLS), lambda b, c: (b, c, 0)),
                  pl.BlockSpec((1, D), lambda b, c: (0, 0))],
        out_specs=pl.BlockSpec((R, L, D), lambda b, c: (b, c, 0)),
        out_shape=jax.ShapeDtypeStruct((batch, seq, D), bf16),
        scratch_shapes=[pltpu.VMEM((nstreams, MLSTM_VROWS, MLSTM_HEAD_DIM), f32),
                        pltpu.VMEM((nstreams, SUBLANES, LANES), f32)],
        compiler_params=pltpu.CompilerParams(
            dimension_semantics=("arbitrary", "arbitrary"), vmem_limit_bytes=VMEM_LIMIT),
        name="mlstm",
    )(p3, p3, p3, g.reshape(batch, seq, GATE_COLS), norm_w)
    return out.reshape(M, D)


def _fox_kernel(q_ref, k_ref, v_ref, f_ref, qw_ref, kw_ref, out_ref,
                qa_ref, ka_ref, vt_ref, acc_ref, m_ref):
    T = FOX_TILE
    Dh = FOX_HEAD_DIM
    S = q_ref.shape[0]
    nt = S // T

    r = lax.broadcasted_iota(jnp.int32, (LANES, LANES), 0) // Dh
    c = lax.broadcasted_iota(jnp.int32, (LANES, LANES), 1) // Dh
    head_mean = jnp.where(r == c, 1.0 / Dh, 0.0).astype(bf16)
    rr = lax.broadcasted_iota(jnp.int32, (3 * LANES, LANES), 0)
    cc = lax.broadcasted_iota(jnp.int32, (3 * LANES, LANES), 1)
    lane = lax.broadcasted_iota(jnp.int32, (S, LANES), 1)
    ones_rows = jnp.where(lax.broadcasted_iota(jnp.int32, (FOX_VROWS - Dh, S), 0) == 0, 1.0, 0.0)

    def head_rms(a, w):
        return a * lax.rsqrt(_dot((a * a).astype(bf16), head_mean) + EPS) * w

    def prepare(sub):
        ls = slice(sub * LANES, (sub + 1) * LANES)
        pair = pl.program_id(1) * FOX_PAIRS_PER_STEP + sub
        qn = head_rms(q_ref[:, ls].astype(f32), qw_ref[...] * (Dh ** -0.5 * LOG2E))
        kn = head_rms(k_ref[:, ls].astype(f32), kw_ref[...])
        head = rr % LANES - 2 * MLSTM_HEADS - pair * FOX_PAIR
        place = ((head >= 0) & (head < FOX_PAIR) & (cc == (1 - head) * Dh + rr // LANES)).astype(bf16)
        bias = _dot(f_ref[...], place)
        v_t = v_ref[:, ls].astype(f32).T
        for hh in range(FOX_PAIR):
            hd = sub * FOX_PAIR + hh
            own = lane // Dh == hh
            b0 = (1 - hh) * Dh
            ones = jnp.where((lane >= b0) & (lane < b0 + 3), 1.0, 0.0)
            qa_ref[hd] = jnp.where(own, qn, ones).astype(bf16)
            ka_ref[hd] = jnp.where(own, kn, bias).astype(bf16)
            vt_ref[hd] = jnp.concatenate([v_t[hh * Dh:(hh + 1) * Dh, :], ones_rows], axis=0).astype(bf16)

    def logits(item):
        j, i, hd = item
        s = _dot_nt(ka_ref[hd, j * T:(j + 1) * T, :], qa_ref[hd, i * T:(i + 1) * T, :])
        if i == j:
            kk = lax.broadcasted_iota(jnp.int32, (T, T), 0)
            qq = lax.broadcasted_iota(jnp.int32, (T, T), 1)
            s = jnp.where(kk <= qq, s, -jnp.inf)
        return s

    def accumulate(item, s):
        j, i, hd = item
        cols = slice(i * T, (i + 1) * T)
        st = slice(hd * nt + i, hd * nt + i + 1)
        m_cur = jnp.max(s, axis=0, keepdims=True)
        if j == 0:
            m_new = m_cur
        else:
            m_old = m_ref[st, :]
            m_new = jnp.maximum(m_old, m_cur)
            alpha = jnp.exp2(m_old - m_new)
        p = jnp.exp2((s - m_new).astype(bf16))
        pv = _dot(vt_ref[hd, :, j * T:(j + 1) * T], p)[0:FOX_AROWS, :]
        if j == 0:
            acc_ref[hd, :, cols] = pv
        else:
            acc_ref[hd, :, cols] = alpha * acc_ref[hd, :, cols] + pv
        m_ref[st, :] = m_new

    for sub in range(FOX_PAIRS_PER_STEP):
        prepare(sub)

    items = [(j, i, sub * FOX_PAIR + hh) for sub in range(FOX_PAIRS_PER_STEP)
             for j in range(nt) for i in range(j, nt) for hh in range(FOX_PAIR)]
    pending = []
    for n in range(len(items) + FOX_LOOKAHEAD):
        if n < len(items):
            pending.append((items[n], logits(items[n])))
        if n >= FOX_LOOKAHEAD:
            accumulate(*pending.pop(0))

    for sub in range(FOX_PAIRS_PER_STEP):
        for i in range(nt):
            cols = slice(i * T, (i + 1) * T)
            o_t = jnp.concatenate(
                [acc_ref[hd, 0:Dh, cols] / acc_ref[hd, Dh:Dh + 1, cols]
                 for hd in range(sub * FOX_PAIR, (sub + 1) * FOX_PAIR)], axis=0)
            out_ref[cols, sub * LANES:(sub + 1) * LANES] = o_t.T.astype(bf16)


def _fox(pf, fcum, q_norm_w, k_norm_w, batch, seq):
    M = pf.shape[1]
    T = FOX_TILE
    nt = seq // T
    nsteps = FOX_HEADS // (FOX_PAIR * FOX_PAIRS_PER_STEP)
    heads = FOX_PAIR * FOX_PAIRS_PER_STEP
    W = FOX_WIDTH

    def pspec(idx):
        return pl.BlockSpec((None, seq, W), lambda b, pr: (idx * nsteps + pr, b, 0))

    return pl.pallas_call(
        _fox_kernel,
        grid=(batch, nsteps),
        in_specs=[pspec(0), pspec(1), pspec(2),
                  pl.BlockSpec((seq, 3 * GATE_COLS), lambda b, pr: (b, 0)),
                  pl.BlockSpec((1, LANES), lambda b, pr: (0, 0)),
                  pl.BlockSpec((1, LANES), lambda b, pr: (0, 0))],
        out_specs=pl.BlockSpec((None, seq, W), lambda b, pr: (pr, b, 0)),
        out_shape=jax.ShapeDtypeStruct((nsteps, M, W), bf16),
        scratch_shapes=[pltpu.VMEM((heads, seq, LANES), bf16),
                        pltpu.VMEM((heads, seq, LANES), bf16),
                        pltpu.VMEM((heads, FOX_VROWS, seq), bf16),
                        pltpu.VMEM((heads, FOX_AROWS, seq), f32),
                        pltpu.VMEM((heads * nt, T), f32)],
        compiler_params=pltpu.CompilerParams(
            dimension_semantics=("arbitrary", "arbitrary"), vmem_limit_bytes=VMEM_LIMIT),
        name="fox",
    )(pf, pf, pf, fcum, q_norm_w, k_norm_w)


def _merge_kernel(x_ref, ada_ref, hm_ref, mo_ref, hf_ref, ga_ref, gb_ref, wa_ref, wb_ref, wo_ref, out_ref):
    hm = hm_ref[...].astype(f32) * jax.nn.sigmoid(mo_ref[...].astype(f32))
    ya = _dot(hm.astype(bf16), wa_ref[...])
    hf = jnp.concatenate([hf_ref[s] for s in range(hf_ref.shape[0])], axis=1)
    yb = _dot(hf, wb_ref[...])
    merged = (jax.nn.sigmoid(ga_ref[...].astype(f32)) * ya
              + jax.nn.sigmoid(gb_ref[...].astype(f32)) * yb)
    y = _dot(merged.astype(bf16), wo_ref[...])
    out_ref[...] = x_ref[...] + ada_ref[2:3, :] * y


def _merge(x2, ada3, hm, hf, p, w_a, w_b, w_o, seq, tm):
    M, D = x2.shape
    tiles_per_seq = seq // tm
    row = pl.BlockSpec((tm, D), lambda m: (m, 0))
    wspec = pl.BlockSpec((D, D), lambda m: (0, 0), pipeline_mode=pl.Buffered(1))
    return pl.pallas_call(
        _merge_kernel,
        grid=(M // tm,),
        in_specs=[row,
                  pl.BlockSpec((None, 6, D), lambda m: (m // tiles_per_seq, 0, 0)),
                  row,
                  pl.BlockSpec((tm, D), lambda m: (m, 3)),
                  pl.BlockSpec((hf.shape[0], tm, hf.shape[2]), lambda m: (0, m, 0)),
                  pl.BlockSpec((tm, D), lambda m: (m, 4)),
                  pl.BlockSpec((tm, D), lambda m: (m, 5)),
                  wspec, wspec, wspec],
        out_specs=row,
        out_shape=jax.ShapeDtypeStruct((M, D), f32),
        compiler_params=pltpu.CompilerParams(
            dimension_semantics=("arbitrary",), vmem_limit_bytes=VMEM_LIMIT),
        name="merge",
    )(x2, ada3, hm, p, hf, p, p, w_a, w_b, w_o)


def _ffn_kernel(x_ref, ada_ref, nw_ref, wup_ref, cw_ref, cb_ref, wdown_ref, out_ref,
                h_ref, tail_ref, acc_ref, *ubuf_refs, tiles_per_seq, tf):
    ts = x_ref.shape[0] // FFN_SUBTILES
    H = SUBLANES
    nf = FFN_DIM // tf

    @pl.when(pl.program_id(0) == 0)
    def _():
        tail_ref[...] = jnp.zeros_like(tail_ref)

    first = (pl.program_id(0) % tiles_per_seq) == 0

    def rows(sub):
        return slice(sub * ts, (sub + 1) * ts)

    def up(n, sub, f):
        if f == 0:
            h_ref[rows(sub), :] = _rms_modulate(
                x_ref[rows(sub), :], nw_ref[...], ada_ref[3:4, :], ada_ref[4:5, :]).astype(bf16)
        for part in range(2):
            cols = slice(part * FFN_DIM + f * tf, part * FFN_DIM + (f + 1) * tf)
            buf = ubuf_refs[2 * (n % FFN_SLOTS) + part]
            u = _dot(h_ref[rows(sub), :], wup_ref[:, cols])
            buf[H:H + ts, :] = u
            halo = tail_ref[:, cols]
            buf[0:H, :] = jnp.where(first, 0.0, halo) if sub == 0 else halo
            tail_ref[:, cols] = u[ts - H:ts, :]

    def conv(n, f, part):
        cols = slice(part * FFN_DIM + f * tf, part * FFN_DIM + (f + 1) * tf)
        buf = ubuf_refs[2 * (n % FFN_SLOTS) + part]
        return (buf[H:H + ts, :] * cw_ref[2:3, cols] + buf[H - 1:H - 1 + ts, :] * cw_ref[1:2, cols]
                + buf[H - 2:H - 2 + ts, :] * cw_ref[0:1, cols] + cb_ref[:, cols])

    def down(n, sub, f):
        ug = conv(n, f, 0)
        a = ((ug * jax.nn.sigmoid(ug)) * conv(n, f, 1)).astype(bf16)
        y = _dot(a, wdown_ref[f * tf:(f + 1) * tf, :])
        if f == 0:
            acc_ref[rows(sub), :] = y
        else:
            acc_ref[rows(sub), :] += y
        if f == nf - 1:
            out_ref[rows(sub), :] = x_ref[rows(sub), :] + ada_ref[5:6, :] * acc_ref[rows(sub), :]

    items = [(sub, f) for sub in range(FFN_SUBTILES) for f in range(nf)]
    for n in range(len(items) + FFN_LOOKAHEAD):
        if n < len(items):
            up(n, *items[n])
        if n >= FFN_LOOKAHEAD:
            down(n - FFN_LOOKAHEAD, *items[n - FFN_LOOKAHEAD])


def _ffn(x2, ada3, norm_w, w_up, conv_w, conv_b, w_down, seq, tm, tf):
    M, D = x2.shape
    F2 = w_up.shape[1]
    tiles_per_seq = seq // tm
    row = pl.BlockSpec((tm, D), lambda m: (m, 0))

    def const(shape):
        return pl.BlockSpec(shape, lambda m: (0, 0), pipeline_mode=pl.Buffered(1))

    return pl.pallas_call(
        functools.partial(_ffn_kernel, tiles_per_seq=tiles_per_seq, tf=tf),
        grid=(M // tm,),
        in_specs=[row,
                  pl.BlockSpec((None, 6, D), lambda m: (m // tiles_per_seq, 0, 0)),
                  const((1, D)), const((D, F2)), const((CONV_WIDTH, F2)), const((1, F2)),
                  const((F2 // 2, D))],
        out_specs=row,
        out_shape=jax.ShapeDtypeStruct((M, D), f32),
        scratch_shapes=[pltpu.VMEM((tm, D), bf16),
                        pltpu.VMEM((SUBLANES, F2), f32),
                        pltpu.VMEM((tm, D), f32)]
        + [pltpu.VMEM((tm // FFN_SUBTILES + SUBLANES, tf), f32) for _ in range(2 * FFN_SLOTS)],
        compiler_params=pltpu.CompilerParams(
            dimension_semantics=("arbitrary",), vmem_limit_bytes=VMEM_LIMIT),
        name="ffn",
    )(x2, ada3, norm_w, w_up, conv_w, conv_b, w_down)


def _layer(x2, c, batch, seq, w_ada, b_ada, norm1_w, w_in, b_mlstm_i, b_mlstm_f, mlstm_norm_w,
           b_fox_f, fox_q_norm_w, fox_k_norm_w, w_branch_mlstm, w_branch_fox, w_out,
           norm2_w, w_up, conv_w, conv_b, w_down):
    D = D_MODEL
    ada3 = _ada(c, w_ada, b_ada.reshape(1, -1)).reshape(batch, 6, D)

    o_mi = 4 * D
    o_fq = o_mi + 2 * MLSTM_HEADS
    o_ff = o_fq + 3 * D
    o_ga = o_ff + FOX_HEADS
    w_t = jnp.swapaxes(w_in, 0, 1)
    w_groups = [w_t[:o_mi].astype(bf16), w_t[o_fq:o_ff].astype(bf16), w_t[o_ga:].astype(bf16)]
    n_gate = 2 * MLSTM_HEADS + FOX_HEADS
    w_gate = jnp.concatenate([w_t[o_mi:o_fq], w_t[o_ff:o_ga],
                              jnp.zeros((GATE_COLS - n_gate, D), f32)], axis=0).astype(bf16)
    b_gate = jnp.concatenate([b_mlstm_i, b_mlstm_f, b_fox_f,
                              jnp.zeros((GATE_COLS - n_gate,), f32)]).reshape(1, GATE_COLS)

    p, pf, g = _in_proj(x2, ada3, norm1_w.reshape(1, D), w_groups, w_gate, b_gate, seq, tm=512)
    fcum = _fox_cum(g, batch, seq)
    hm = _mlstm(p, g, mlstm_norm_w.reshape(1, D), batch, seq)
    hf = _fox(pf, fcum, jnp.tile(fox_q_norm_w, FOX_PAIR).reshape(1, LANES),
              jnp.tile(fox_k_norm_w, FOX_PAIR).reshape(1, LANES), batch, seq)
    x2 = _merge(x2, ada3, hm, hf, p, w_branch_mlstm.astype(bf16), w_branch_fox.astype(bf16),
                w_out.astype(bf16), seq, tm=512)
    return _ffn(x2, ada3, norm2_w.reshape(1, D), w_up.astype(bf16), conv_w, conv_b.reshape(1, -1),
                w_down.astype(bf16), seq, tm=512 * FFN_SUBTILES, tf=256)


def kernel(x, c, w_ada, b_ada, norm1_w, w_in, b_mlstm_i, b_mlstm_f, mlstm_norm_w, b_fox_f,
           fox_q_norm_w, fox_k_norm_w, w_branch_mlstm, w_branch_fox, w_out, norm2_w, w_up,
           conv_w, conv_b, w_down):
    batch, seq, D = x.shape
    x2 = x.reshape(batch * seq, D)
    for l in range(w_ada.shape[0]):
        x2 = _layer(x2, c, batch, seq, w_ada[l], b_ada[l], norm1_w[l], w_in[l], b_mlstm_i[l],
                    b_mlstm_f[l], mlstm_norm_w[l], b_fox_f[l], fox_q_norm_w[l], fox_k_norm_w[l],
                    w_branch_mlstm[l], w_branch_fox[l], w_out[l], norm2_w[l], w_up[l],
                    conv_w[l], conv_b[l], w_down[l])
    return x2.reshape(batch, seq, D)
```

```python
import functools

import jax
import jax.numpy as jnp
from jax import lax
from jax.experimental import pallas as pl
from jax.experimental.pallas import tpu as pltpu

D_MODEL = 1024
MLSTM_HEADS = 4
MLSTM_HEAD_DIM = 256
FOX_HEADS = 16
FOX_HEAD_DIM = 64
FFN_DIM = 2816
CONV_WIDTH = 3
EPS = 1e-6

LANES = 128
SUBLANES = 8
GATE_COLS = LANES
MLSTM_CHUNK = 256
MLSTM_ROWS = 4
MLSTM_VROWS = MLSTM_HEAD_DIM + 16
FOX_TILE = 256
FOX_PAIR = LANES // FOX_HEAD_DIM
FOX_PAIRS_PER_STEP = 2
FOX_WIDTH = LANES * FOX_PAIRS_PER_STEP
FOX_VROWS = FOX_HEAD_DIM + 16
FOX_AROWS = FOX_HEAD_DIM + SUBLANES
FOX_LOOKAHEAD = 5
FFN_LOOKAHEAD = 3
FFN_SLOTS = FFN_LOOKAHEAD + 1
FFN_SUBTILES = 1
VMEM_LIMIT = 56 * 1024 * 1024
LOG2E = 1.4426950408889634

f32 = jnp.float32
bf16 = jnp.bfloat16


def _dot(a, b):
    return jnp.dot(a, b, preferred_element_type=f32)


def _dot_nt(a, b):
    return lax.dot_general(a, b, (((1,), (1,)), ((), ())), preferred_element_type=f32)


def _dot_tn(a, b):
    return lax.dot_general(a, b, (((0,), (0,)), ((), ())), preferred_element_type=f32)


def _split3(a):
    a1 = a.astype(bf16)
    r1 = a - a1.astype(f32)
    a2 = r1.astype(bf16)
    a3 = (r1 - a2.astype(f32)).astype(bf16)
    return a1, a2, a3


def _log_sigmoid(x):
    return jnp.minimum(x, 0.0) - jnp.log1p(jnp.exp(-jnp.abs(x)))


def _rms_modulate(x, norm_w, shift, scale):
    y = x * lax.rsqrt(jnp.mean(x * x, axis=-1, keepdims=True) + EPS)
    return (y * norm_w) * (1.0 + scale) + shift


def _ada_kernel(c_ref, w_ref, b_ref, out_ref):
    c = c_ref[...]
    a = (c * jax.nn.sigmoid(c)).astype(bf16)
    out_ref[...] = _dot(a, w_ref[...].astype(bf16)) + b_ref[...]


def _ada(c, w_ada, b_ada):
    B, D = c.shape
    n = w_ada.shape[1] // D
    return pl.pallas_call(
        _ada_kernel,
        grid=(n,),
        in_specs=[pl.BlockSpec((B, D), lambda j: (0, 0)),
                  pl.BlockSpec((D, D), lambda j: (0, j)),
                  pl.BlockSpec((1, D), lambda j: (0, j))],
        out_specs=pl.BlockSpec((B, D), lambda j: (0, j)),
        out_shape=jax.ShapeDtypeStruct((B, n * D), f32),
        name="ada",
    )(c, w_ada, b_ada)


def _in_proj_kernel(x_ref, ada_ref, nw_ref, wm_ref, wf_ref, wb_ref, wg_ref, bg_ref,
                    p_ref, pf_ref, g_ref, f_ref, h_ref, carry_ref, *, tiles_per_seq):
    @pl.when(pl.program_id(0) == 0)
    def _():
        carry_ref[...] = jnp.zeros_like(carry_ref)

    D = x_ref.shape[1]
    W = pf_ref.shape[2]
    h = _rms_modulate(x_ref[...], nw_ref[...], ada_ref[0:1, :], ada_ref[1:2, :]).astype(bf16)
    h_ref[...] = h
    g = _dot_nt(h_ref[...], wg_ref[...]) + bg_ref[...]
    col = lax.broadcasted_iota(jnp.int32, g.shape, 1)
    g = jnp.where(col < MLSTM_HEADS, g, _log_sigmoid(g))
    g_ref[...] = g

    def gate_cumsum():
        T = FOX_TILE
        row = lax.broadcasted_iota(jnp.int32, (T, T), 0)
        lower = (lax.broadcasted_iota(jnp.int32, (T, T), 1) <= row).astype(bf16)
        first = (pl.program_id(0) % tiles_per_seq) == 0
        carry = jnp.where(first, 0.0, carry_ref[0:1, :])
        for i in range(g.shape[0] // T):
            a1, a2, a3 = _split3(g[i * T:(i + 1) * T, :])
            cs = (_dot(lower, a1) + _dot(lower, a2)) + _dot(lower, a3) + carry
            f_ref[i * T:(i + 1) * T, :] = jnp.concatenate(_split3(cs * (-LOG2E)), axis=1)
            carry = cs[T - 1:T, :]
        carry_ref[0:1, :] = carry

    j = 0
    for w_ref in (wm_ref, wb_ref):
        for jj in range(w_ref.shape[0] // D):
            p_ref[:, j * D:(j + 1) * D] = _dot_nt(h_ref[...], w_ref[jj * D:(jj + 1) * D, :]).astype(bf16)
            j += 1
            if j == 2:
                gate_cumsum()
    for jj in range(wf_ref.shape[0] // D):
        y = _dot_nt(h_ref[...], wf_ref[jj * D:(jj + 1) * D, :]).astype(bf16)
        for s in range(D // W):
            pf_ref[jj * (D // W) + s] = y[:, s * W:(s + 1) * W]


def _in_proj(x2, ada3, norm_w, w_groups, w_gate, b_gate, seq, tm):
    M, D = x2.shape
    tiles_per_seq = seq // tm
    n_main = (w_groups[0].shape[0] + w_groups[2].shape[0]) // D
    n_slab = w_groups[1].shape[0] // FOX_WIDTH

    def const(shape):
        return pl.BlockSpec(shape, lambda m: (0, 0), pipeline_mode=pl.Buffered(1))

    return pl.pallas_call(
        functools.partial(_in_proj_kernel, tiles_per_seq=tiles_per_seq),
        grid=(M // tm,),
        in_specs=[pl.BlockSpec((tm, D), lambda m: (m, 0)),
                  pl.BlockSpec((None, 6, D), lambda m: (m // tiles_per_seq, 0, 0)),
                  const((1, D))] + [const(w.shape) for w in w_groups]
        + [const((GATE_COLS, D)), const((1, GATE_COLS))],
        out_specs=[pl.BlockSpec((tm, n_main * D), lambda m: (m, 0)),
                   pl.BlockSpec((n_slab, tm, FOX_WIDTH), lambda m: (0, m, 0)),
                   pl.BlockSpec((tm, GATE_COLS), lambda m: (m, 0)),
                   pl.BlockSpec((tm, 3 * GATE_COLS), lambda m: (m, 0))],
        out_shape=[jax.ShapeDtypeStruct((M, n_main * D), bf16),
                   jax.ShapeDtypeStruct((n_slab, M, FOX_WIDTH), bf16),
                   jax.ShapeDtypeStruct((M, GATE_COLS), f32),
                   jax.ShapeDtypeStruct((M, 3 * GATE_COLS), bf16)],
        scratch_shapes=[pltpu.VMEM((tm, D), bf16), pltpu.VMEM((SUBLANES, GATE_COLS), f32)],
        compiler_params=pltpu.CompilerParams(
            dimension_semantics=("arbitrary",), vmem_limit_bytes=VMEM_LIMIT),
        name="in_proj",
    )(x2, ada3, norm_w, *w_groups, w_gate, b_gate)


def _mlstm_kernel(q_ref, k_ref, v_ref, g_ref, nw_ref, out_ref, ct_ref, m_ref):
    L = MLSTM_CHUNK
    Dh = MLSTM_HEAD_DIM
    streams = [(r, h) for r in range(MLSTM_ROWS) for h in range(MLSTM_HEADS)]

    @pl.when(pl.program_id(1) == 0)
    def _():
        ct_ref[...] = jnp.zeros_like(ct_ref)
        m_ref[...] = jnp.zeros_like(m_ref)

    src_pos = lax.broadcasted_iota(jnp.int32, (L, L), 0)
    qry_pos = lax.broadcasted_iota(jnp.int32, (L, L), 1)
    visible = src_pos <= qry_pos
    lower = (qry_pos <= src_pos).astype(bf16)
    upper = visible.astype(bf16)
    ones_rows = jnp.where(lax.broadcasted_iota(jnp.int32, (MLSTM_VROWS - Dh, L), 0) == 0, 1.0, 0.0)

    def hs(h):
        return slice(h * Dh, (h + 1) * Dh)

    gs, gts, b_cols, b_rows = [], [], [], []
    for r in range(MLSTM_ROWS):
        g = g_ref[r]
        g1, g2, g3 = _split3(g)
        b_cols.append((_dot(lower, g1) + _dot(lower, g2)) + _dot(lower, g3))
        gt = g.T[0:SUBLANES, :]
        t1, t2, t3 = _split3(gt)
        b_rows.append((_dot(t1, upper) + _dot(t2, upper)) + _dot(t3, upper))
        gs.append(g)
        gts.append(gt)

    qs = [q_ref[r, :, hs(h)] * jnp.asarray(Dh ** -0.5, bf16) for r, h in streams]
    vts = [jnp.concatenate([v_ref[r, :, hs(h)].astype(f32).T, ones_rows], axis=0).astype(bf16)
           for r, h in streams]

    qk_t = [_dot_nt(k_ref[r, :, hs(h)], qs[st]) for st, (r, h) in enumerate(streams)]
    qc_t = [_dot_nt(ct_ref[st].astype(bf16), qs[st]) for st in range(len(streams))]

    d_intra, d_inter, m_t = [], [], []
    for st, (r, h) in enumerate(streams):
        i_row = gts[r][h:h + 1, :]
        b_row = b_rows[r][MLSTM_HEADS + h:MLSTM_HEADS + h + 1, :]
        a_col = gs[r][:, h:h + 1] - b_cols[r][:, MLSTM_HEADS + h:MLSTM_HEADS + h + 1]
        m_prev = m_ref[st][0:1, 0:1]

        log_intra = jnp.where(visible, b_row + a_col, -jnp.inf)
        log_inter = b_row + m_prev
        m_h = jnp.maximum(log_inter, jnp.max(log_intra, axis=0, keepdims=True))
        m_t.append(m_h)
        d_intra.append(jnp.exp(log_intra - m_h))
        d_inter.append(jnp.exp(log_inter - m_h))

        b_last = b_row[:, L - 1:L]
        log_w = b_last - b_row + i_row
        m_new = jnp.maximum(b_last + m_prev, jnp.max(log_w, axis=1, keepdims=True))
        w_row = jnp.exp(log_w - m_new)
        decay = jnp.exp(b_last + m_prev - m_new)
        vw = vts[st] * w_row.astype(bf16)
        ct_ref[st] = decay * ct_ref[st] + _dot(vw, k_ref[r, :, hs(h)])
        m_ref[st] = jnp.broadcast_to(m_new, m_ref.shape[1:])

    for st, (r, h) in enumerate(streams):
        p_t = (qk_t[st] * d_intra[st]).astype(bf16)
        tot = _dot(vts[st], p_t) + d_inter[st] * qc_t[st]
        den = tot[Dh:Dh + 1, :]
        hh = tot[0:Dh, :] * (1.0 / jnp.maximum(jnp.abs(den), jnp.exp(-m_t[st])))
        hh = hh * lax.rsqrt(jnp.mean(hh * hh, axis=0, keepdims=True) + EPS)
        out_ref[r, :, hs(h)] = (hh.T * nw_ref[:, hs(h)]).astype(bf16)


def _mlstm(p, g, norm_w, batch, seq):
    M, D = p.shape[0], D_MODEL
    L = MLSTM_CHUNK
    R = MLSTM_ROWS
    nstreams = R * MLSTM_HEADS
    p3 = p.reshape(batch, seq, p.shape[1])

    def pspec(idx):
        return pl.BlockSpec((R, L, D), lambda b, c: (b, c, idx))

    out = pl.pallas_call(
        _mlstm_kernel,
        grid=(batch // R, seq // L),
        in_specs=[pspec(0), pspec(1), pspec(2),
                  pl.BlockSpec((R, L, GATE_COLS), lambda b, c: (b, c, 0)),
                  pl.BlockSpec((1, D), lambda b, c: (0, 0))],
        out_specs=pl.BlockSpec((R, L, D), lambda b, c: (b, c, 0)),
        out_shape=jax.ShapeDtypeStruct((batch, seq, D), bf16),
        scratch_shapes=[pltpu.VMEM((nstreams, MLSTM_VROWS, MLSTM_HEAD_DIM), f32),
                        pltpu.VMEM((nstreams, SUBLANES, LANES), f32)],
        compiler_params=pltpu.CompilerParams(
            dimension_semantics=("arbitrary", "arbitrary"), vmem_limit_bytes=VMEM_LIMIT),
        name="mlstm",
    )(p3, p3, p3, g.reshape(batch, seq, GATE_COLS), norm_w)
    return out.reshape(M, D)


def _fox_kernel(q_ref, k_ref, v_ref, f_ref, qw_ref, kw_ref, out_ref,
                qa_ref, ka_ref, vt_ref, acc_ref, m_ref):
    T = FOX_TILE
    Dh = FOX_HEAD_DIM
    S = q_ref.shape[0]
    nt = S // T

    r = lax.broadcasted_iota(jnp.int32, (LANES, LANES), 0) // Dh
    c = lax.broadcasted_iota(jnp.int32, (LANES, LANES), 1) // Dh
    head_mean = jnp.where(r == c, 1.0 / Dh, 0.0).astype(bf16)
    rr = lax.broadcasted_iota(jnp.int32, (3 * LANES, LANES), 0)
    cc = lax.broadcasted_iota(jnp.int32, (3 * LANES, LANES), 1)
    lane = lax.broadcasted_iota(jnp.int32, (S, LANES), 1)
    ones_rows = jnp.where(lax.broadcasted_iota(jnp.int32, (FOX_VROWS - Dh, S), 0) == 0, 1.0, 0.0)

    def head_rms(a, w):
        return a * lax.rsqrt(_dot((a * a).astype(bf16), head_mean) + EPS) * w

    def prepare(sub):
        ls = slice(sub * LANES, (sub + 1) * LANES)
        pair = pl.program_id(1) * FOX_PAIRS_PER_STEP + sub
        qn = head_rms(q_ref[:, ls].astype(f32), qw_ref[...] * (Dh ** -0.5 * LOG2E))
        kn = head_rms(k_ref[:, ls].astype(f32), kw_ref[...])
        head = rr % LANES - 2 * MLSTM_HEADS - pair * FOX_PAIR
        place = ((head >= 0) & (head < FOX_PAIR) & (cc == (1 - head) * Dh + rr // LANES)).astype(bf16)
        bias = _dot(f_ref[...], place)
        v_t = v_ref[:, ls].astype(f32).T
        for hh in range(FOX_PAIR):
            hd = sub * FOX_PAIR + hh
            own = lane // Dh == hh
            b0 = (1 - hh) * Dh
            ones = jnp.where((lane >= b0) & (lane < b0 + 3), 1.0, 0.0)
            qa_ref[hd] = jnp.where(own, qn, ones).astype(bf16)
            ka_ref[hd] = jnp.where(own, kn, bias).astype(bf16)
            vt_ref[hd] = jnp.concatenate([v_t[hh * Dh:(hh + 1) * Dh, :], ones_rows], axis=0).astype(bf16)

    def logits(item):
        j, i, hd = item
        s = _dot_nt(ka_ref[hd, j * T:(j + 1) * T, :], qa_ref[hd, i * T:(i + 1) * T, :])
        if i == j:
            kk = lax.broadcasted_iota(jnp.int32, (T, T), 0)
            qq = lax.broadcasted_iota(jnp.int32, (T, T), 1)
            s = jnp.where(kk <= qq, s, -jnp.inf)
        return s

    def accumulate(item, s):
        j, i, hd = item
        cols = slice(i * T, (i + 1) * T)
        st = slice(hd * nt + i, hd * nt + i + 1)
        m_cur = jnp.max(s, axis=0, keepdims=True)
        if j == 0:
            m_new = m_cur
        else:
            m_old = m_ref[st, :]
            m_new = jnp.maximum(m_old, m_cur)
            alpha = jnp.exp2(m_old - m_new)
        p = jnp.exp2((s - m_new).astype(bf16))
        pv = _dot(vt_ref[hd, :, j * T:(j + 1) * T], p)[0:FOX_AROWS, :]
        if j == 0:
            acc_ref[hd, :, cols] = pv
        else:
            acc_ref[hd, :, cols] = alpha * acc_ref[hd, :, cols] + pv
        m_ref[st, :] = m_new

    for sub in range(FOX_PAIRS_PER_STEP):
        prepare(sub)

    items = [(j, i, sub * FOX_PAIR + hh) for sub in range(FOX_PAIRS_PER_STEP)
             for j in range(nt) for i in range(j, nt) for hh in range(FOX_PAIR)]
    pending = []
    for n in range(len(items) + FOX_LOOKAHEAD):
        if n < len(items):
            pending.append((items[n], logits(items[n])))
        if n >= FOX_LOOKAHEAD:
            accumulate(*pending.pop(0))

    for sub in range(FOX_PAIRS_PER_STEP):
        for i in range(nt):
            cols = slice(i * T, (i + 1) * T)
            o_t = jnp.concatenate(
                [acc_ref[hd, 0:Dh, cols] / acc_ref[hd, Dh:Dh + 1, cols]
                 for hd in range(sub * FOX_PAIR, (sub + 1) * FOX_PAIR)], axis=0)
            out_ref[cols, sub * LANES:(sub + 1) * LANES] = o_t.T.astype(bf16)


def _fox(pf, fcum, q_norm_w, k_norm_w, batch, seq):
    M = pf.shape[1]
    T = FOX_TILE
    nt = seq // T
    nsteps = FOX_HEADS // (FOX_PAIR * FOX_PAIRS_PER_STEP)
    heads = FOX_PAIR * FOX_PAIRS_PER_STEP
    W = FOX_WIDTH

    def pspec(idx):
        return pl.BlockSpec((None, seq, W), lambda b, pr: (idx * nsteps + pr, b, 0))

    return pl.pallas_call(
        _fox_kernel,
        grid=(batch, nsteps),
        in_specs=[pspec(0), pspec(1), pspec(2),
                  pl.BlockSpec((seq, 3 * GATE_COLS), lambda b, pr: (b, 0)),
                  pl.BlockSpec((1, LANES), lambda b, pr: (0, 0)),
                  pl.BlockSpec((1, LANES), lambda b, pr: (0, 0))],
        out_specs=pl.BlockSpec((None, seq, W), lambda b, pr: (pr, b, 0)),
        out_shape=jax.ShapeDtypeStruct((nsteps, M, W), bf16),
        scratch_shapes=[pltpu.VMEM((heads, seq, LANES), bf16),
                        pltpu.VMEM((heads, seq, LANES), bf16),
                        pltpu.VMEM((heads, FOX_VROWS, seq), bf16),
                        pltpu.VMEM((heads, FOX_AROWS, seq), f32),
                        pltpu.VMEM((heads * nt, T), f32)],
        compiler_params=pltpu.CompilerParams(
            dimension_semantics=("arbitrary", "arbitrary"), vmem_limit_bytes=VMEM_LIMIT),
        name="fox",
    )(pf, pf, pf, fcum, q_norm_w, k_norm_w)


def _merge_kernel(x_ref, ada_ref, hm_ref, mo_ref, hf_ref, ga_ref, gb_ref, wa_ref, wb_ref, wo_ref, out_ref):
    hm = hm_ref[...].astype(f32) * jax.nn.sigmoid(mo_ref[...].astype(f32))
    ya = _dot(hm.astype(bf16), wa_ref[...])
    hf = jnp.concatenate([hf_ref[s] for s in range(hf_ref.shape[0])], axis=1)
    yb = _dot(hf, wb_ref[...])
    merged = (jax.nn.sigmoid(ga_ref[...].astype(f32)) * ya
              + jax.nn.sigmoid(gb_ref[...].astype(f32)) * yb)
    y = _dot(merged.astype(bf16), wo_ref[...])
    out_ref[...] = x_ref[...] + ada_ref[2:3, :] * y


def _merge(x2, ada3, hm, hf, p, w_a, w_b, w_o, seq, tm):
    M, D = x2.shape
    tiles_per_seq = seq // tm
    row = pl.BlockSpec((tm, D), lambda m: (m, 0))
    wspec = pl.BlockSpec((D, D), lambda m: (0, 0), pipeline_mode=pl.Buffered(1))
    return pl.pallas_call(
        _merge_kernel,
        grid=(M // tm,),
        in_specs=[row,
                  pl.BlockSpec((None, 6, D), lambda m: (m // tiles_per_seq, 0, 0)),
                  row,
                  pl.BlockSpec((tm, D), lambda m: (m, 3)),
                  pl.BlockSpec((hf.shape[0], tm, hf.shape[2]), lambda m: (0, m, 0)),
                  pl.BlockSpec((tm, D), lambda m: (m, 4)),
                  pl.BlockSpec((tm, D), lambda m: (m, 5)),
                  wspec, wspec, wspec],
        out_specs=row,
        out_shape=jax.ShapeDtypeStruct((M, D), f32),
        compiler_params=pltpu.CompilerParams(
            dimension_semantics=("arbitrary",), vmem_limit_bytes=VMEM_LIMIT),
        name="merge",
    )(x2, ada3, hm, p, hf, p, p, w_a, w_b, w_o)


def _ffn_kernel(x_ref, ada_ref, nw_ref, wup_ref, cw_ref, cb_ref, wdown_ref, out_ref,
                h_ref, tail_ref, acc_ref, *ubuf_refs, tiles_per_seq, tf):
    ts = x_ref.shape[0] // FFN_SUBTILES
    H = SUBLANES
    nf = FFN_DIM // tf

    @pl.when(pl.program_id(0) == 0)
    def _():
        tail_ref[...] = jnp.zeros_like(tail_ref)

    first = (pl.program_id(0) % tiles_per_seq) == 0

    def rows(sub):
        return slice(sub * ts, (sub + 1) * ts)

    def up(n, sub, f):
        if f == 0:
            h_ref[rows(sub), :] = _rms_modulate(
                x_ref[rows(sub), :], nw_ref[...], ada_ref[3:4, :], ada_ref[4:5, :]).astype(bf16)
        for part in range(2):
            cols = slice(part * FFN_DIM + f * tf, part * FFN_DIM + (f + 1) * tf)
            buf = ubuf_refs[2 * (n % FFN_SLOTS) + part]
            u = _dot(h_ref[rows(sub), :], wup_ref[:, cols])
            buf[H:H + ts, :] = u
            halo = tail_ref[:, cols]
            buf[0:H, :] = jnp.where(first, 0.0, halo) if sub == 0 else halo
            tail_ref[:, cols] = u[ts - H:ts, :]

    def conv(n, f, part):
        cols = slice(part * FFN_DIM + f * tf, part * FFN_DIM + (f + 1) * tf)
        buf = ubuf_refs[2 * (n % FFN_SLOTS) + part]
        return (buf[H:H + ts, :] * cw_ref[2:3, cols] + buf[H - 1:H - 1 + ts, :] * cw_ref[1:2, cols]
                + buf[H - 2:H - 2 + ts, :] * cw_ref[0:1, cols] + cb_ref[:, cols])

    def down(n, sub, f):
        ug = conv(n, f, 0)
        a = ((ug * jax.nn.sigmoid(ug)) * conv(n, f, 1)).astype(bf16)
        y = _dot(a, wdown_ref[f * tf:(f + 1) * tf, :])
        if f == 0:
            acc_ref[rows(sub), :] = y
        else:
            acc_ref[rows(sub), :] += y
        if f == nf - 1:
            out_ref[rows(sub), :] = x_ref[rows(sub), :] + ada_ref[5:6, :] * acc_ref[rows(sub), :]

    items = [(sub, f) for sub in range(FFN_SUBTILES) for f in range(nf)]
    for n in range(len(items) + FFN_LOOKAHEAD):
        if n < len(items):
            up(n, *items[n])
        if n >= FFN_LOOKAHEAD:
            down(n - FFN_LOOKAHEAD, *items[n - FFN_LOOKAHEAD])


def _ffn(x2, ada3, norm_w, w_up, conv_w, conv_b, w_down, seq, tm, tf):
    M, D = x2.shape
    F2 = w_up.shape[1]
    tiles_per_seq = seq // tm
    row = pl.BlockSpec((tm, D), lambda m: (m, 0))

    def const(shape):
        return pl.BlockSpec(shape, lambda m: (0, 0), pipeline_mode=pl.Buffered(1))

    return pl.pallas_call(
        functools.partial(_ffn_kernel, tiles_per_seq=tiles_per_seq, tf=tf),
        grid=(M // tm,),
        in_specs=[row,
                  pl.BlockSpec((None, 6, D), lambda m: (m // tiles_per_seq, 0, 0)),
                  const((1, D)), const((D, F2)), const((CONV_WIDTH, F2)), const((1, F2)),
                  const((F2 // 2, D))],
        out_specs=row,
        out_shape=jax.ShapeDtypeStruct((M, D), f32),
        scratch_shapes=[pltpu.VMEM((tm, D), bf16),
                        pltpu.VMEM((SUBLANES, F2), f32),
                        pltpu.VMEM((tm, D), f32)]
        + [pltpu.VMEM((tm // FFN_SUBTILES + SUBLANES, tf), f32) for _ in range(2 * FFN_SLOTS)],
        compiler_params=pltpu.CompilerParams(
            dimension_semantics=("arbitrary",), vmem_limit_bytes=VMEM_LIMIT),
        name="ffn",
    )(x2, ada3, norm_w, w_up, conv_w, conv_b, w_down)


def _layer(x2, c, batch, seq, w_ada, b_ada, norm1_w, w_in, b_mlstm_i, b_mlstm_f, mlstm_norm_w,
           b_fox_f, fox_q_norm_w, fox_k_norm_w, w_branch_mlstm, w_branch_fox, w_out,
           norm2_w, w_up, conv_w, conv_b, w_down):
    D = D_MODEL
    ada3 = _ada(c, w_ada, b_ada.reshape(1, -1)).reshape(batch, 6, D)

    o_mi = 4 * D
    o_fq = o_mi + 2 * MLSTM_HEADS
    o_ff = o_fq + 3 * D
    o_ga = o_ff + FOX_HEADS
    w_t = jnp.swapaxes(w_in, 0, 1)
    w_groups = [w_t[:o_mi].astype(bf16), w_t[o_fq:o_ff].astype(bf16), w_t[o_ga:].astype(bf16)]
    n_gate = 2 * MLSTM_HEADS + FOX_HEADS
    w_gate = jnp.concatenate([w_t[o_mi:o_fq], w_t[o_ff:o_ga],
                              jnp.zeros((GATE_COLS - n_gate, D), f32)], axis=0).astype(bf16)
    b_gate = jnp.concatenate([b_mlstm_i, b_mlstm_f, b_fox_f,
                              jnp.zeros((GATE_COLS - n_gate,), f32)]).reshape(1, GATE_COLS)

    p, pf, g, fcum = _in_proj(x2, ada3, norm1_w.reshape(1, D), w_groups, w_gate, b_gate, seq, tm=512)
    hm = _mlstm(p, g, mlstm_norm_w.reshape(1, D), batch, seq)
    hf = _fox(pf, fcum, jnp.tile(fox_q_norm_w, FOX_PAIR).reshape(1, LANES),
              jnp.tile(fox_k_norm_w, FOX_PAIR).reshape(1, LANES), batch, seq)
    x2 = _merge(x2, ada3, hm, hf, p, w_branch_mlstm.astype(bf16), w_branch_fox.astype(bf16),
                w_out.astype(bf16), seq, tm=512)
    return _ffn(x2, ada3, norm2_w.reshape(1, D), w_up.astype(bf16), conv_w, conv_b.reshape(1, -1),
                w_down.astype(bf16), seq, tm=512 * FFN_SUBTILES, tf=256)


def kernel(x, c, w_ada, b_ada, norm1_w, w_in, b_mlstm_i, b_mlstm_f, mlstm_norm_w, b_fox_f,
           fox_q_norm_w, fox_k_norm_w, w_branch_mlstm, w_branch_fox, w_out, norm2_w, w_up,
           conv_w, conv_b, w_down):
    batch, seq, D = x.shape
    x2 = x.reshape(batch * seq, D)
    for l in range(w_ada.shape[0]):
        x2 = _layer(x2, c, batch, seq, w_ada[l], b_ada[l], norm1_w[l], w_in[l], b_mlstm_i[l],
                    b_mlstm_f[l], mlstm_norm_w[l], b_fox_f[l], fox_q_norm_w[l], fox_k_norm_w[l],
                    w_branch_mlstm[l], w_branch_fox[l], w_out[l], norm2_w[l], w_up[l],
                    conv_w[l], conv_b[l], w_down[l])
    return x2.reshape(batch, seq, D)
```
